```python
import math
import jax, jax.numpy as jnp
from jax import lax
import numpy as np

D_MODEL = 4096
BATCH = 4
SEQ = 2048
DEPTH = 4
DEC_BATCH = 128
DEC_SEQ = 8
PAST_LEN = 16384
PAGE_SIZE = 128

N_MIXERS = 2
N_SC_LAYERS = (DEPTH + 1) // 2
N_SSD_LAYERS = DEPTH // 2
SC_WIDTH = 3
SSD_EXPAND = 2
SSD_D_INNER = SSD_EXPAND * D_MODEL
SSD_HEAD_DIM = 64
SSD_HEADS = SSD_D_INNER // SSD_HEAD_DIM
SSD_GROUPS = 8
SSD_HEADS_PER_GROUP = SSD_HEADS // SSD_GROUPS
SSD_D_STATE = 128
SSD_CONV_WIDTH = 4
SSD_BC_DIM = SSD_GROUPS * SSD_D_STATE
SSD_CONV_DIM = SSD_D_INNER + 2 * SSD_BC_DIM
SSD_IN_DIM = SSD_D_INNER + SSD_CONV_DIM + SSD_HEADS
SSD_CHUNK = 128
DT_MIN = 0.001
DT_MAX = 0.1
FFN_HIDDEN = ((8 * D_MODEL + 3 * 256 - 1) // (3 * 256)) * 256
DEEPNORM_ALPHA = (2 * DEPTH) ** 0.25
DEEPNORM_BETA = (8 * DEPTH) ** -0.25
LN_EPS = 1e-5
RMS_EPS = 1e-5

kernel_name = 'hybrid_shortconv_ssd_adaln_deepnorm_step'


def layer_norm(x, g, b):
    xf = x.astype(jnp.float32)
    mu = jnp.mean(xf, axis=-1, keepdims=True)
    xc = xf - mu
    var = jnp.mean(jnp.square(xc), axis=-1, keepdims=True)
    return (xc * lax.rsqrt(var + LN_EPS) * g.astype(jnp.float32) + b.astype(jnp.float32)).astype(x.dtype)


def causal_depthwise_conv(u, buf, w):
    K = w.shape[0]
    L = u.shape[1]
    full = jnp.concatenate([buf.astype(u.dtype), u], axis=1)
    y = full[:, 0:L] * w[0]
    for k in range(1, K):
        y = y + full[:, k:k + L] * w[k]
    return y, full[:, L:]


def short_conv_mixer(h, buf, w_in, conv_w, w_out):
    proj = jnp.einsum('bld,de->ble', h, w_in)
    gate_b, gate_c, v = jnp.split(proj, 3, axis=-1)
    u = gate_c * v
    conv, new_buf = causal_depthwise_conv(u, buf, conv_w)
    return jnp.einsum('ble,ed->bld', gate_b * conv, w_out), new_buf


def ssd_chunk_scan(x, a, bm, cm, s0):
    b, L = x.shape[0], x.shape[1]
    q = SSD_CHUNK if L % SSD_CHUNK == 0 else L
    nc = L // q
    G, R, P, N = SSD_GROUPS, SSD_HEADS_PER_GROUP, SSD_HEAD_DIM, SSD_D_STATE
    xg = jnp.moveaxis(x.reshape(b, nc, q, G, R, P), 1, 0)
    ag = jnp.moveaxis(a.reshape(b, nc, q, G, R), 1, 0)
    bg = jnp.moveaxis(bm.reshape(b, nc, q, G, N), 1, 0)
    cg = jnp.moveaxis(cm.reshape(b, nc, q, G, N), 1, 0)
    causal = jnp.tril(jnp.ones((q, q), dtype=bool))[None, :, :, None, None]

    def step(s, inp):
        xc, ac, bc, cc = inp
        cs = jnp.cumsum(ac, axis=1)
        diff = cs[:, :, None] - cs[:, None, :]
        lmat = jnp.exp(jnp.where(causal, diff, -jnp.inf))
        cb = jnp.einsum('blgn,bsgn->blsg', cc, bc)
        y_diag = jnp.einsum('blsg,blsgr,bsgrp->blgrp', cb, lmat, xc)
        y_off = jnp.einsum('blgn,bgrpn->blgrp', cc, s) * jnp.exp(cs)[..., None]
        decay = jnp.exp(cs[:, -1:] - cs)
        s_new = s * jnp.exp(cs[:, -1])[..., None, None] + jnp.einsum('bsgn,bsgr,bsgrp->bgrpn', bc, decay, xc)
        return s_new, y_diag + y_off

    s_fin, ys = lax.scan(step, s0.reshape(b, G, R, P, N), (xg, ag, bg, cg))
    y = jnp.moveaxis(ys, 0, 1).reshape(b, L, SSD_HEADS, P)
    return y, s_fin.reshape(b, SSD_HEADS, P, N)


def ssd_mixer(h, buf, state, w_in, conv_w, conv_b, dt_bias, a_log, d_skip, norm_g, w_out):
    b, L, _ = h.shape
    f32 = jnp.float32
    proj = jnp.einsum('bld,de->ble', h, w_in)
    z = proj[..., :SSD_D_INNER]
    xbc = proj[..., SSD_D_INNER:SSD_D_INNER + SSD_CONV_DIM]
    dt_raw = proj[..., SSD_D_INNER + SSD_CONV_DIM:]
    xbc, new_buf = causal_depthwise_conv(xbc, buf, conv_w)
    xbc = jax.nn.silu((xbc + conv_b).astype(f32))
    xs = xbc[..., :SSD_D_INNER].reshape(b, L, SSD_HEADS, SSD_HEAD_DIM)
    bm = xbc[..., SSD_D_INNER:SSD_D_INNER + SSD_BC_DIM].reshape(b, L, SSD_GROUPS, SSD_D_STATE)
    cm = xbc[..., SSD_D_INNER + SSD_BC_DIM:].reshape(b, L, SSD_GROUPS, SSD_D_STATE)
    dt = jax.nn.softplus(dt_raw.astype(f32) + dt_bias.astype(f32))
    a = -jnp.exp(a_log.astype(f32))
    y, new_state = ssd_chunk_scan(xs * dt[..., None], dt * a, bm, cm, state.astype(f32))
    y = y + d_skip.astype(f32)[:, None] * xs
    y = y.reshape(b, L, SSD_D_INNER) * jax.nn.silu(z.astype(f32))
    yg = y.reshape(b, L, SSD_GROUPS, SSD_D_INNER // SSD_GROUPS)
    yg = yg * lax.rsqrt(jnp.mean(jnp.square(yg), axis=-1, keepdims=True) + RMS_EPS)
    y = yg.reshape(b, L, SSD_D_INNER) * norm_g.astype(f32)
    out = jnp.einsum('ble,ed->bld', y.astype(h.dtype), w_out)
    return out, new_buf, new_state


def swiglu(h, w_gate, w_up, w_down):
    hid = jax.nn.silu(jnp.einsum('bld,df->blf', h, w_gate)) * jnp.einsum('bld,df->blf', h, w_up)
    return jnp.einsum('blf,fd->bld', hid, w_down)


def trunk(x, c, sc_buf, ssd_buf, ssd_state, w_ada, b_ada, ln_g, ln_b,
          sc_w_in, sc_conv_w, sc_w_out,
          ssd_w_in, ssd_conv_w, ssd_conv_b, ssd_dt_bias, ssd_a_log, ssd_d, ssd_norm_g, ssd_w_out,
          ffn_w_gate, ffn_w_up, ffn_w_down):
    sc_new, ssd_buf_new, ssd_state_new = [], [], []
    c_act = jax.nn.silu(c)
    for i in range(DEPTH):
        mod = jnp.einsum('bd,de->be', c_act, w_ada[i]) + b_ada[i]
        sh1, sc1, g1, sh2, sc2, g2 = [m[:, None, :] for m in jnp.split(mod, 6, axis=-1)]
        h = x * (1 + sc1) + sh1
        j = i // N_MIXERS
        if i % N_MIXERS == 0:
            out, nb = short_conv_mixer(h, sc_buf[j], sc_w_in[j], sc_conv_w[j], sc_w_out[j])
            sc_new.append(nb)
        else:
            out, nb, ns = ssd_mixer(h, ssd_buf[j], ssd_state[j], ssd_w_in[j], ssd_conv_w[j], ssd_conv_b[j],
                                    ssd_dt_bias[j], ssd_a_log[j], ssd_d[j], ssd_norm_g[j], ssd_w_out[j])
            ssd_buf_new.append(nb)
            ssd_state_new.append(ns)
        x = layer_norm(DEEPNORM_ALPHA * x + g1 * out, ln_g[i, 0], ln_b[i, 0])
        h = x * (1 + sc2) + sh2
        x = layer_norm(DEEPNORM_ALPHA * x + g2 * swiglu(h, ffn_w_gate[i], ffn_w_up[i], ffn_w_down[i]),
                       ln_g[i, 1], ln_b[i, 1])
    return x, jnp.stack(sc_new), jnp.stack(ssd_buf_new), jnp.stack(ssd_state_new)


def setup_inputs(seed: int = 0) -> dict:
    key = jax.random.key(seed)
    ks = jax.random.split(key, 32)
    f32 = jnp.float32

    def nrm(k, shape, s):
        return jax.random.normal(k, shape, f32) * s

    dt = jnp.exp(jax.random.uniform(ks[20], (N_SSD_LAYERS, SSD_HEADS), f32,
                                    minval=math.log(DT_MIN), maxval=math.log(DT_MAX)))
    return {
        'x_prompt': nrm(ks[0], (BATCH, SEQ, D_MODEL), 1.0),
        'x_sample': nrm(ks[1], (DEC_BATCH, DEC_SEQ, D_MODEL), 1.0),
        'state_sconv': nrm(ks[2], (N_SC_LAYERS, DEC_BATCH, SC_WIDTH - 1, D_MODEL), 1.0),
        'state_ssd_conv': nrm(ks[3], (N_SSD_LAYERS, DEC_BATCH, SSD_CONV_WIDTH - 1, SSD_CONV_DIM), 1.0),
        'state_ssd': nrm(ks[4], (N_SSD_LAYERS, DEC_BATCH, SSD_HEADS, SSD_HEAD_DIM, SSD_D_STATE), 0.3),
        'c_prompt': nrm(ks[5], (BATCH, D_MODEL), 1.0),
        'c_sample': nrm(ks[6], (DEC_BATCH, D_MODEL), 1.0),
        'w_ada': nrm(ks[7], (DEPTH, D_MODEL, 6 * D_MODEL), 0.5 * D_MODEL ** -0.5),
        'b_ada': nrm(ks[8], (DEPTH, 6 * D_MODEL), 0.01),
        'ln_g': 1.0 + nrm(ks[9], (DEPTH, 2, D_MODEL), 0.02),
        'ln_b': nrm(ks[10], (DEPTH, 2, D_MODEL), 0.02),
        'sc_w_in': nrm(ks[11], (N_SC_LAYERS, D_MODEL, 3 * D_MODEL), D_MODEL ** -0.5),
        'sc_conv_w': nrm(ks[12], (N_SC_LAYERS, SC_WIDTH, D_MODEL), SC_WIDTH ** -0.5),
        'sc_w_out': nrm(ks[13], (N_SC_LAYERS, D_MODEL, D_MODEL), DEEPNORM_BETA * D_MODEL ** -0.5),
        'ssd_w_in': nrm(ks[14], (N_SSD_LAYERS, D_MODEL, SSD_IN_DIM), D_MODEL ** -0.5),
        'ssd_conv_w': nrm(ks[15], (N_SSD_LAYERS, SSD_CONV_WIDTH, SSD_CONV_DIM), SSD_CONV_WIDTH ** -0.5),
        'ssd_conv_b': nrm(ks[16], (N_SSD_LAYERS, SSD_CONV_DIM), 0.01),
        'ssd_dt_bias': dt + jnp.log(-jnp.expm1(-dt)),
        'ssd_a_log': jnp.log(jax.random.uniform(ks[17], (N_SSD_LAYERS, SSD_HEADS), f32, minval=1.0, maxval=16.0)),
        'ssd_d': 1.0 + nrm(ks[18], (N_SSD_LAYERS, SSD_HEADS), 0.02),
        'ssd_norm_g': 1.0 + nrm(ks[19], (N_SSD_LAYERS, SSD_D_INNER), 0.02),
        'ssd_w_out': nrm(ks[21], (N_SSD_LAYERS, SSD_D_INNER, D_MODEL), DEEPNORM_BETA * SSD_D_INNER ** -0.5),
        'ffn_w_gate': nrm(ks[22], (DEPTH, D_MODEL, FFN_HIDDEN), D_MODEL ** -0.5),
        'ffn_w_up': nrm(ks[23], (DEPTH, D_MODEL, FFN_HIDDEN), D_MODEL ** -0.5),
        'ffn_w_down': nrm(ks[24], (DEPTH, FFN_HIDDEN, D_MODEL), DEEPNORM_BETA * FFN_HIDDEN ** -0.5),
    }


def reference(x_prompt, x_sample, state_sconv, state_ssd_conv, state_ssd, c_prompt, c_sample,
              w_ada, b_ada, ln_g, ln_b, sc_w_in, sc_conv_w, sc_w_out,
              ssd_w_in, ssd_conv_w, ssd_conv_b, ssd_dt_bias, ssd_a_log, ssd_d, ssd_norm_g, ssd_w_out,
              ffn_w_gate, ffn_w_up, ffn_w_down):
    bp = x_prompt.shape[0]
    zero_sc = jnp.zeros((N_SC_LAYERS, bp, SC_WIDTH - 1, D_MODEL), x_prompt.dtype)
    zero_ssd_conv = jnp.zeros((N_SSD_LAYERS, bp, SSD_CONV_WIDTH - 1, SSD_CONV_DIM), x_prompt.dtype)
    zero_ssd = jnp.zeros((N_SSD_LAYERS, bp, SSD_HEADS, SSD_HEAD_DIM, SSD_D_STATE), jnp.float32)
    y_prompt, sc_p, ssd_conv_p, ssd_p = trunk(
        x_prompt, c_prompt, zero_sc, zero_ssd_conv, zero_ssd, w_ada, b_ada, ln_g, ln_b,
        sc_w_in, sc_conv_w, sc_w_out, ssd_w_in, ssd_conv_w, ssd_conv_b, ssd_dt_bias, ssd_a_log,
        ssd_d, ssd_norm_g, ssd_w_out, ffn_w_gate, ffn_w_up, ffn_w_down)
    y_sample, sc_s, ssd_conv_s, ssd_s = trunk(
        x_sample, c_sample, state_sconv, state_ssd_conv, state_ssd, w_ada, b_ada, ln_g, ln_b,
        sc_w_in, sc_conv_w, sc_w_out, ssd_w_in, ssd_conv_w, ssd_conv_b, ssd_dt_bias, ssd_a_log,
        ssd_d, ssd_norm_g, ssd_w_out, ffn_w_gate, ffn_w_up, ffn_w_down)
    return (y_prompt, y_sample, sc_p, ssd_conv_p, ssd_p, sc_s, ssd_conv_s, ssd_s)
```

```python
import functools
import math

import jax
import jax.numpy as jnp
from jax import lax
from jax.experimental import pallas as pl
from jax.experimental.pallas import tpu as pltpu

F32 = jnp.float32
BF16 = jnp.bfloat16
HIGHEST = lax.Precision.HIGHEST

LN_EPS = 1e-5
RMS_EPS = 1e-5
SSD_CHUNK = 128
N_MOD = 6

V7X_VMEM_LIMIT_BYTES = 56 * 1024 * 1024
LANE = 128
SUBLANE = 8
FIX_ROWS = 2 * SUBLANE


def _pick(dim, pref, align):
    best = None
    t = align
    while t <= min(dim, pref):
        if dim % t == 0:
            best = t
        t += align
    return best if best is not None else dim


def _params(sem):
    return pltpu.CompilerParams(dimension_semantics=sem, vmem_limit_bytes=V7X_VMEM_LIMIT_BYTES)


def _silu(x):
    return x * jax.nn.sigmoid(x)


def _mm_kernel(*refs, n_w, n_out, nk, epilogue):
    a_ref = refs[0]
    w_refs = refs[1:1 + n_w]
    o_refs = refs[1 + n_w:1 + n_w + n_out]
    acc_refs = refs[1 + n_w + n_out:]

    def finish(accs):
        if epilogue == "plain":
            o_refs[0][...] = accs[0].astype(o_refs[0].dtype)
        elif epilogue == "swiglu":
            o_refs[0][...] = (_silu(accs[0]) * accs[1]).astype(o_refs[0].dtype)
        else:
            o_refs[0][...] = accs[0].astype(o_refs[0].dtype)
            o_refs[1][...] = (accs[1] * accs[2]).astype(o_refs[1].dtype)

    a = a_ref[...]
    parts = [jnp.dot(a, w[...], preferred_element_type=F32) for w in w_refs]
    if nk == 1:
        finish(parts)
        return
    k = pl.program_id(2)

    @pl.when(k == 0)
    def _():
        for acc, p in zip(acc_refs, parts):
            acc[...] = p

    @pl.when(k > 0)
    def _():
        for acc, p in zip(acc_refs, parts):
            acc[...] += p

    @pl.when(k == nk - 1)
    def _():
        finish([acc[...] for acc in acc_refs])


def _matmul(a, ws, col_offsets, n, out_dtypes, *, epilogue, tm, tn, tk, name):
    m, kdim = a.shape
    tm = _pick(m, tm, 16)
    tn = _pick(n, tn, LANE)
    tk = _pick(kdim, tk, LANE)
    nk = kdim // tk
    for off in col_offsets:
        assert off % tn == 0
    n_w = len(ws)
    n_out = len(out_dtypes)
    in_specs = [pl.BlockSpec((tm, tk), lambda i, j, k: (i, k))]
    for off in col_offsets:
        in_specs.append(pl.BlockSpec((tk, tn), functools.partial(lambda i, j, k, o: (k, j + o), o=off // tn)))
    out_specs = [pl.BlockSpec((tm, tn), lambda i, j, k: (i, j)) for _ in out_dtypes]
    scratch = [pltpu.VMEM((tm, tn), F32) for _ in ws] if nk > 1 else []
    outs = pl.pallas_call(
        functools.partial(_mm_kernel, n_w=n_w, n_out=n_out, nk=nk, epilogue=epilogue),
        grid=(m // tm, n // tn, nk),
        in_specs=in_specs,
        out_specs=out_specs,
        out_shape=[jax.ShapeDtypeStruct((m, n), dt) for dt in out_dtypes],
        scratch_shapes=scratch,
        compiler_params=_params(("parallel", "parallel", "arbitrary")),
        name=name,
    )(a, *ws)
    return outs


def _ada_kernel(c_ref, w_ref, b_ref, o_ref):
    w = w_ref[...].astype(BF16)
    o_ref[...] = jnp.dot(c_ref[...], w, preferred_element_type=F32) + b_ref[...]


def _adaln(c_act, w_ada, b_ada):
    depth, d, n = w_ada.shape
    rows = c_act.shape[0]
    tn = _pick(n, 512, LANE)
    return pl.pallas_call(
        _ada_kernel,
        grid=(depth, n // tn),
        in_specs=[
            pl.BlockSpec((rows, d), lambda l, j: (0, 0)),
            pl.BlockSpec((None, d, tn), lambda l, j: (l, 0, j)),
            pl.BlockSpec((None, 1, tn), lambda l, j: (l, 0, j)),
        ],
        out_specs=pl.BlockSpec((None, rows, tn), lambda l, j: (l, 0, j)),
        out_shape=jax.ShapeDtypeStruct((depth, rows, n), F32),
        compiler_params=_params(("parallel", "parallel")),
        name="adaln",
    )(c_act, w_ada, b_ada.reshape(depth, 1, n))


def _mod_specs(tm, d, n_p, tiles_per_seq, n_batch, chunk):
    p_spec = pl.BlockSpec((None, 1, d), lambda i: (jnp.minimum(i // tiles_per_seq, n_batch - 1), 0, chunk))
    s_spec = pl.BlockSpec((tm, d), lambda i: (jnp.maximum(i - n_p, 0), chunk))
    return p_spec, s_spec


def _modulate_kernel(x_ref, scp_ref, shp_ref, scs_ref, shs_ref, h_ref, *, n_p):
    i = pl.program_id(0)

    @pl.when(i < n_p)
    def _():
        h_ref[...] = (x_ref[...] * (1.0 + scp_ref[...]) + shp_ref[...]).astype(h_ref.dtype)

    @pl.when(i >= n_p)
    def _():
        h_ref[...] = (x_ref[...] * (1.0 + scs_ref[...]) + shs_ref[...]).astype(h_ref.dtype)


def _ln_kernel(*refs, n_p, alpha, with_next):
    if with_next:
        (x_ref, o_ref, gp_ref, gs_ref, lg_ref, lb_ref, scp_ref, shp_ref, scs_ref, shs_ref,
         xo_ref, h_ref) = refs
    else:
        x_ref, o_ref, gp_ref, gs_ref, lg_ref, lb_ref, xo_ref = refs
    i = pl.program_id(0)

    def body(g, sc, sh):
        v = alpha * x_ref[...] + g * o_ref[...]
        mu = jnp.mean(v, axis=-1, keepdims=True)
        vc = v - mu
        var = jnp.mean(vc * vc, axis=-1, keepdims=True)
        xn = vc * lax.rsqrt(var + LN_EPS) * lg_ref[...] + lb_ref[...]
        xo_ref[...] = xn
        if with_next:
            h_ref[...] = (xn * (1.0 + sc) + sh).astype(h_ref.dtype)

    @pl.when(i < n_p)
    def _():
        body(gp_ref[...], scp_ref[...] if with_next else None, shp_ref[...] if with_next else None)

    @pl.when(i >= n_p)
    def _():
        body(gs_ref[...], scs_ref[...] if with_next else None, shs_ref[...] if with_next else None)


class _Slab:
    def __init__(self, batch, seq, dec_batch, dec_seq, d):
        self.batch, self.seq, self.dec_batch, self.dec_seq, self.d = batch, seq, dec_batch, dec_seq, d
        self.m_p = batch * seq
        self.m_s = dec_batch * dec_seq
        self.m = self.m_p + self.m_s
        self.tm = _pick(math.gcd(seq, self.m_s), 128, math.lcm(dec_seq, 16))
        self.n_p = self.m_p // self.tm
        self.n_t = self.m // self.tm
        self.tiles_per_seq = seq // self.tm


def _modulate(g, x, mod_p, mod_s, chunk_sc, chunk_sh):
    tm, d = g.tm, g.d
    scp, scs = _mod_specs(tm, d, g.n_p, g.tiles_per_seq, g.batch, chunk_sc)
    shp, shs = _mod_specs(tm, d, g.n_p, g.tiles_per_seq, g.batch, chunk_sh)
    row = pl.BlockSpec((tm, d), lambda i: (i, 0))
    return pl.pallas_call(
        functools.partial(_modulate_kernel, n_p=g.n_p),
        grid=(g.n_t,),
        in_specs=[row, scp, shp, scs, shs],
        out_specs=row,
        out_shape=jax.ShapeDtypeStruct((g.m, d), BF16),
        compiler_params=_params(("parallel",)),
        name="modulate",
    )(x, mod_p, mod_p, mod_s, mod_s)


def _ln_mod(g, x, o, mod_p, mod_s, chunk_g, ln_g, ln_b, alpha, nxt):
    tm, d = g.tm, g.d
    row = pl.BlockSpec((tm, d), lambda i: (i, 0))
    vec = pl.BlockSpec((1, d), lambda i: (0, 0))
    gp, gs = _mod_specs(tm, d, g.n_p, g.tiles_per_seq, g.batch, chunk_g)
    in_specs = [row, row, gp, gs, vec, vec]
    args = [x, o, mod_p, mod_s, ln_g.reshape(1, d), ln_b.reshape(1, d)]
    out_specs = [row]
    out_shape = [jax.ShapeDtypeStruct((g.m, d), F32)]
    if nxt is not None:
        nmod_p, nmod_s, c_sc, c_sh = nxt
        scp, scs = _mod_specs(tm, d, g.n_p, g.tiles_per_seq, g.batch, c_sc)
        shp, shs = _mod_specs(tm, d, g.n_p, g.tiles_per_seq, g.batch, c_sh)
        in_specs += [scp, shp, scs, shs]
        args += [nmod_p, nmod_p, nmod_s, nmod_s]
        out_specs.append(row)
        out_shape.append(jax.ShapeDtypeStruct((g.m, d), BF16))
    outs = pl.pallas_call(
        functools.partial(_ln_kernel, n_p=g.n_p, alpha=alpha, with_next=nxt is not None),
        grid=(g.n_t,),
        in_specs=in_specs,
        out_specs=out_specs,
        out_shape=out_shape,
        compiler_params=_params(("parallel",)),
        name="ln_mod",
    )(*args)
    return outs if nxt is not None else (outs[0], None)


def _conv_kernel(*refs, n_p, tiles_per_seq, width, dec_seq, with_bias_silu, with_gate):
    u_ref, halo_ref = refs[0], refs[1]
    e_refs = refs[2:2 + width - 1]
    pos = 2 + width - 1
    w_ref = refs[pos]
    pos += 1
    b_ref = g_ref = None
    if with_bias_silu:
        b_ref = refs[pos]
        pos += 1
    if with_gate:
        g_ref = refs[pos]
        pos += 1
    o_ref = refs[pos]
    i = pl.program_id(0)
    w = w_ref[...]

    def post(y, rows):
        if with_bias_silu:
            y = _silu(y + b_ref[...])
        if with_gate:
            y = y * g_ref[rows, :]
        return y.astype(o_ref.dtype)

    u = u_ref[...]
    tm = u.shape[0]

    @pl.when(i < n_p)
    def _():
        y = u * w[width - 1:width, :]
        for s in range(1, width):
            y = y + pltpu.roll(u, s, 0) * w[width - 1 - s:width - s, :]
        o_ref[...] = post(y, slice(None))
        keep = jnp.where(i % tiles_per_seq == 0, 0.0, 1.0)
        halo = halo_ref[...] * keep
        halo = jnp.concatenate([halo, halo], axis=0)
        top = u[0:FIX_ROWS, :]
        r = lax.broadcasted_iota(jnp.int32, top.shape, 0)
        yt = top * w[width - 1:width, :]
        for s in range(1, width):
            sh = jnp.where(r < s, pltpu.roll(halo, s, 0), pltpu.roll(top, s, 0))
            yt = yt + sh * w[width - 1 - s:width - s, :]
        o_ref[0:FIX_ROWS, :] = post(yt, slice(0, FIX_ROWS))

    @pl.when(i >= n_p)
    def _():
        t = lax.broadcasted_iota(jnp.int32, u.shape, 0) % dec_seq
        y = u * w[width - 1:width, :]
        for s in range(1, width):
            sh = jnp.where(t >= s, pltpu.roll(u, s, 0), e_refs[s - 1][...])
            y = y + sh * w[width - 1 - s:width - s, :]
        o_ref[...] = post(y, slice(None))


def _state_rows(buf, s, dec_seq):
    n_seq, km1, c = buf.shape
    assert dec_seq >= km1
    rows = buf[:, km1 - s:, :]
    rows = jnp.pad(rows, ((0, 0), (0, dec_seq - s), (0, 0)))
    return rows.reshape(n_seq * dec_seq, c)


def _causal_conv(g, u, buf_s, w, bias, gate, out_dtype, name):
    m, c = u.shape
    width = w.shape[0]
    tm = g.tm
    tc = _pick(c, 1024, LANE)
    n_p = g.n_p
    es = [_state_rows(buf_s, s, g.dec_seq) for s in range(1, width)]
    tile = pl.BlockSpec((tm, tc), lambda i, j: (i, j))
    halo = pl.BlockSpec((SUBLANE, tc), lambda i, j: (jnp.maximum(i * (tm // SUBLANE) - 1, 0), j))
    e_spec = pl.BlockSpec((tm, tc), lambda i, j: (jnp.maximum(i - n_p, 0), j))
    in_specs = [tile, halo] + [e_spec] * (width - 1) + [pl.BlockSpec((width, tc), lambda i, j: (0, j))]
    args = [u, u] + es + [w]
    if bias is not None:
        in_specs.append(pl.BlockSpec((1, tc), lambda i, j: (0, j)))
        args.append(bias.reshape(1, c))
    if gate is not None:
        in_specs.append(tile)
        args.append(gate)
    return pl.pallas_call(
        functools.partial(_conv_kernel, n_p=n_p, tiles_per_seq=g.tiles_per_seq, width=width,
                          dec_seq=g.dec_seq, with_bias_silu=bias is not None, with_gate=gate is not None),
        grid=(g.n_t, c // tc),
        in_specs=in_specs,
        out_specs=tile,
        out_shape=jax.ShapeDtypeStruct((m, c), out_dtype),
        compiler_params=_params(("parallel", "parallel")),
        name=name,
    )(*args)


def _dt_kernel(r_ref, b_ref, al_ref, dt_ref, adt_ref):
    dt = jax.nn.softplus(r_ref[...] + b_ref[...])
    dt_ref[...] = dt
    adt_ref[...] = dt * (-jnp.exp(al_ref[...]))


def _dt_prep(dt_raw, dt_bias, a_log):
    m, h = dt_raw.shape
    tm = _pick(m, 1024, SUBLANE)
    row = pl.BlockSpec((tm, h), lambda i: (i, 0))
    vec = pl.BlockSpec((1, h), lambda i: (0, 0))
    return pl.pallas_call(
        _dt_kernel,
        grid=(m // tm,),
        in_specs=[row, vec, vec],
        out_specs=[row, row],
        out_shape=[jax.ShapeDtypeStruct((m, h), F32)] * 2,
        compiler_params=_params(("parallel",)),
        name="ssd_dt",
    )(dt_raw, dt_bias.reshape(1, h), a_log.reshape(1, h))


def _ssd_kernel(*refs, q, r_heads, p_dim, zero_init):
    if zero_init:
        (xs_ref, b_ref, c_ref, z_ref, dtc_ref, adtc_ref, adtr_ref, dsk_ref, ng_ref,
         y_ref, s_ref) = refs
        s0_ref = None
    else:
        (xs_ref, b_ref, c_ref, z_ref, dtc_ref, adtc_ref, adtr_ref, dsk_ref, ng_ref, s0_ref,
         y_ref, s_ref) = refs
    ci = pl.program_id(2)
    rp = r_heads * p_dim

    @pl.when(ci == 0)
    def _():
        if zero_init:
            s_ref[...] = jnp.zeros_like(s_ref)
        else:
            s_ref[...] = s0_ref[...]

    xs = xs_ref[...]
    bm = b_ref[...].astype(BF16)
    cm = c_ref[...].astype(BF16)
    dt_c = dtc_ref[...]
    adt_c = adtc_ref[...]
    adt_r = adtr_ref[...]

    row = lax.broadcasted_iota(jnp.int32, (q, q), 0)
    col = lax.broadcasted_iota(jnp.int32, (q, q), 1)
    causal = row >= col
    tril = jnp.where(causal, 1.0, 0.0).astype(F32)
    triu = jnp.where(row <= col, 1.0, 0.0).astype(F32)
    cs_c = jnp.dot(tril, adt_c, precision=HIGHEST, preferred_element_type=F32)
    cs_r = jnp.dot(adt_r, triu, precision=HIGHEST, preferred_element_type=F32)
    cs_last_row = cs_c[q - 1:q, :]
    cs_last_col = cs_r[:, q - 1:q]

    head_of = lax.broadcasted_iota(jnp.int32, (r_heads, rp), 1) // p_dim
    expand = jnp.where(head_of == lax.broadcasted_iota(jnp.int32, (r_heads, rp), 0), 1.0, 0.0).astype(F32)

    def widen(v):
        return jnp.dot(v, expand, precision=HIGHEST, preferred_element_type=F32)

    dt_x = widen(dt_c)
    ecs_x = widen(jnp.exp(cs_c))
    dec_x = widen(jnp.exp(cs_last_row - cs_c))
    n_state = s_ref.shape[-1]
    elast = lax.dot_general(expand, jnp.broadcast_to(jnp.exp(cs_last_col), (r_heads, n_state)),
                            (((0,), (0,)), ((), ())), precision=HIGHEST,
                            preferred_element_type=F32)

    xdt = xs * dt_x
    xb = xdt.astype(BF16)
    s_old = s_ref[...]

    y = lax.dot_general(cm, s_old.astype(BF16), (((1,), (1,)), ((), ())),
                        preferred_element_type=F32) * ecs_x
    s_ref[...] = s_old * elast + lax.dot_general((xdt * dec_x).astype(BF16), bm, (((0,), (0,)), ((), ())),
                                                 preferred_element_type=F32)

    cb = lax.dot_general(cm, bm, (((1,), (1,)), ((), ())), preferred_element_type=F32)
    pieces = []
    heads_per_blk = max(1, LANE // p_dim)
    assert r_heads % heads_per_blk == 0
    for blk in range(r_heads // heads_per_blk):
        width = heads_per_blk * p_dim
        xblk = xb[:, blk * width:(blk + 1) * width]
        lane = lax.broadcasted_iota(jnp.int32, (q, width), 1)
        acc = None
        for k in range(heads_per_blk):
            h = blk * heads_per_blk + k
            diff = cs_c[:, h:h + 1] - cs_r[h:h + 1, :]
            lmat = jnp.exp(jnp.where(causal, diff, -jnp.inf))
            yk = jnp.dot((cb * lmat).astype(BF16), xblk, preferred_element_type=F32)
            acc = yk if acc is None else jnp.where(lane >= k * p_dim, yk, acc)
        pieces.append(acc)
    y = y + (pieces[0] if len(pieces) == 1 else jnp.concatenate(pieces, axis=1))

    y = y + dsk_ref[...] * xs
    yz = y * _silu(z_ref[...])
    ms = jnp.mean(yz * yz, axis=-1, keepdims=True)
    y_ref[...] = (yz * lax.rsqrt(ms + RMS_EPS) * ng_ref[...]).astype(y_ref.dtype)


def _ssd_scan(xbc, z, dt, adt, d_skip_x, norm_g, s0, *, row0, n_seq, seq_len, groups, r_heads, p_dim, n_state):
    q = SSD_CHUNK if seq_len % SSD_CHUNK == 0 else seq_len
    nc = seq_len // q
    rp = r_heads * p_dim
    d_inner = groups * rp
    assert row0 % q == 0 and rp % LANE == 0 and n_state % LANE == 0 and d_inner % n_state == 0
    blk0 = row0 // q
    rows = n_seq * seq_len
    heads = groups * r_heads

    def per_group(v):
        v = v[row0:row0 + rows].reshape(n_seq * nc, q, groups, r_heads)
        return v.transpose(2, 0, 1, 3), v.transpose(2, 0, 3, 1)

    dt_c, _ = per_group(dt)
    adt_c, adt_r = per_group(adt)

    b_col0 = d_inner // n_state
    tok = lambda s, g, c: (blk0 + s * nc + c, g)
    in_specs = [
        pl.BlockSpec((q, rp), tok),
        pl.BlockSpec((q, n_state), lambda s, g, c: (blk0 + s * nc + c, b_col0 + g)),
        pl.BlockSpec((q, n_state), lambda s, g, c: (blk0 + s * nc + c, b_col0 + groups + g)),
        pl.BlockSpec((q, rp), tok),
        pl.BlockSpec((None, None, q, r_heads), lambda s, g, c: (g, s * nc + c, 0, 0)),
        pl.BlockSpec((None, None, q, r_heads), lambda s, g, c: (g, s * nc + c, 0, 0)),
        pl.BlockSpec((None, None, r_heads, q), lambda s, g, c: (g, s * nc + c, 0, 0)),
        pl.BlockSpec((1, rp), lambda s, g, c: (0, g)),
        pl.BlockSpec((1, rp), lambda s, g, c: (0, g)),
    ]
    args = [xbc, xbc, xbc, z, dt_c, adt_c, adt_r, d_skip_x, norm_g.reshape(1, d_inner)]
    state_spec = pl.BlockSpec((None, None, rp, n_state), lambda s, g, c: (s, g, 0, 0))
    if s0 is not None:
        in_specs.append(state_spec)
        args.append(s0.reshape(n_seq, groups, rp, n_state))
    y, s_new = pl.pallas_call(
        functools.partial(_ssd_kernel, q=q, r_heads=r_heads, p_dim=p_dim, zero_init=s0 is None),
        grid=(n_seq, groups, nc),
        in_specs=in_specs,
        out_specs=[pl.BlockSpec((q, rp), lambda s, g, c: (s * nc + c, g)), state_spec],
        out_shape=[jax.ShapeDtypeStruct((rows, d_inner), F32),
                   jax.ShapeDtypeStruct((n_seq, groups, rp, n_state), F32)],
        compiler_params=_params(("parallel", "parallel", "arbitrary")),
        name="ssd_scan_q%d" % q,
    )(*args)
    return y, s_new.reshape(n_seq, heads, p_dim, n_state)


def _tails(u, g, k):
    c = u.shape[1]
    p = u[:g.m_p].reshape(g.batch, g.seq, c)[:, g.seq - k:]
    s = u[g.m_p:].reshape(g.dec_batch, g.dec_seq, c)[:, g.dec_seq - k:]
    return p, s


def kernel(x_prompt, x_sample, state_sconv, state_ssd_conv, state_ssd, c_prompt, c_sample, w_ada, b_ada, ln_g, ln_b, sc_w_in, sc_conv_w, sc_w_out, ssd_w_in, ssd_conv_w, ssd_conv_b, ssd_dt_bias, ssd_a_log, ssd_d, ssd_norm_g, ssd_w_out, ffn_w_gate, ffn_w_up, ffn_w_down):
    batch, seq, d = x_prompt.shape
    dec_batch, dec_seq, _ = x_sample.shape
    depth = w_ada.shape[0]
    g = _Slab(batch, seq, dec_batch, dec_seq, d)
    alpha = (2 * depth) ** 0.25

    heads, p_dim, n_state = state_ssd.shape[2:]
    d_inner = heads * p_dim
    conv_dim = ssd_conv_w.shape[-1]
    groups = (conv_dim - d_inner) // (2 * n_state)
    r_heads = heads // groups
    ffn_hidden = ffn_w_gate.shape[-1]
    ffn_pad = -(-ffn_hidden // 1024) * 1024 if ffn_hidden > 1024 else ffn_hidden

    n_c = batch + dec_batch
    n_c_pad = -(-n_c // 16) * 16
    c_all = jnp.concatenate([c_prompt, c_sample], axis=0)
    c_act = jnp.pad(_silu(c_all), ((0, n_c_pad - n_c), (0, 0))).astype(BF16)
    mod = _adaln(c_act, w_ada, b_ada)
    mod_p = [mod[l, :batch].reshape(batch, 1, N_MOD * d) for l in range(depth)]
    mod_s = [jnp.repeat(mod[l, batch:n_c], dec_seq, axis=0) for l in range(depth)]

    x = jnp.concatenate([x_prompt.reshape(g.m_p, d), x_sample.reshape(g.m_s, d)], axis=0)
    h = _modulate(g, x, mod_p[0], mod_s[0], 1, 0)

    sc_p, sc_s, cv_p, cv_s, st_p, st_s = [], [], [], [], [], []
    for i in range(depth):
        j = i // 2
        if i % 2 == 0:
            w_in = sc_w_in[j].astype(BF16)
            gate_b, u = _matmul(h, [w_in, w_in, w_in], [0, d, 2 * d], d, [F32, F32],
                                epilogue="sconv", tm=512, tn=512, tk=d, name="sconv_in")
            y = _causal_conv(g, u, state_sconv[j], sc_conv_w[j], None, gate_b, BF16, "sconv_conv")
            tp, ts = _tails(u, g, sc_conv_w.shape[1] - 1)
            sc_p.append(tp)
            sc_s.append(ts)
            (out,) = _matmul(y, [sc_w_out[j].astype(BF16)], [0], d, [F32],
                             epilogue="plain", tm=1024, tn=1024, tk=4096, name="sconv_out")
        else:
            w_in = ssd_w_in[j].astype(BF16)
            (z,) = _matmul(h, [w_in], [0], d_inner, [F32],
                           epilogue="plain", tm=1024, tn=1024, tk=4096, name="ssd_in_z")
            (xbc,) = _matmul(h, [w_in], [d_inner], conv_dim, [F32],
                             epilogue="plain", tm=1024, tn=1024, tk=4096, name="ssd_in_xbc")
            (dt_raw,) = _matmul(h, [w_in], [d_inner + conv_dim], heads, [F32],
                                epilogue="plain", tm=1024, tn=heads, tk=4096, name="ssd_in_dt")
            tp, ts = _tails(xbc, g, ssd_conv_w.shape[1] - 1)
            cv_p.append(tp)
            cv_s.append(ts)
            xbc_act = _causal_conv(g, xbc, state_ssd_conv[j], ssd_conv_w[j], ssd_conv_b[j], None, F32, "ssd_conv")
            dt, adt = _dt_prep(dt_raw, ssd_dt_bias[j], ssd_a_log[j])
            d_skip_x = jnp.repeat(ssd_d[j], p_dim).reshape(1, d_inner)
            common = dict(groups=groups, r_heads=r_heads, p_dim=p_dim, n_state=n_state)
            y_p, s_p = _ssd_scan(xbc_act, z, dt, adt, d_skip_x, ssd_norm_g[j], None,
                                 row0=0, n_seq=batch, seq_len=seq, **common)
            y_s, s_s = _ssd_scan(xbc_act, z, dt, adt, d_skip_x, ssd_norm_g[j], state_ssd[j],
                                 row0=g.m_p, n_seq=dec_batch, seq_len=dec_seq, **common)
            st_p.append(s_p)
            st_s.append(s_s)
            y = jnp.concatenate([y_p, y_s], axis=0).astype(BF16)
            (out,) = _matmul(y, [ssd_w_out[j].astype(BF16)], [0], d, [F32],
                             epilogue="plain", tm=1024, tn=1024, tk=4096, name="ssd_out")
        x, h = _ln_mod(g, x, out, mod_p[i], mod_s[i], 2, ln_g[i, 0], ln_b[i, 0], alpha,
                       (mod_p[i], mod_s[i], 4, 3))

        pad = ((0, 0), (0, ffn_pad - ffn_hidden))
        w_gate = jnp.pad(ffn_w_gate[i].astype(BF16), pad)
        w_up = jnp.pad(ffn_w_up[i].astype(BF16), pad)
        w_down = jnp.pad(ffn_w_down[i].astype(BF16), (pad[1], pad[0]))
        (hid,) = _matmul(h, [w_gate, w_up], [0, 0], ffn_pad, [BF16],
                         epilogue="swiglu", tm=1024, tn=512, tk=d, name="ffn_in")
        (out,) = _matmul(hid, [w_down], [0], d, [F32],
                         epilogue="plain", tm=1024, tn=1024, tk=2816, name="ffn_out")
        nxt = (mod_p[i + 1], mod_s[i + 1], 1, 0) if i + 1 < depth else None
        x, h = _ln_mod(g, x, out, mod_p[i], mod_s[i], 5, ln_g[i, 1], ln_b[i, 1], alpha, nxt)

    y_prompt = x[:g.m_p].reshape(batch, seq, d)
    y_sample = x[g.m_p:].reshape(dec_batch, dec_seq, d)
    return (y_prompt, y_sample, jnp.stack(sc_p), jnp.stack(cv_p), jnp.stack(st_p),
            jnp.stack(sc_s), jnp.stack(cv_s), jnp.stack(st_s))
```

```python
import functools
import math

import numpy as np
import jax
import jax.numpy as jnp
from jax import lax
from jax.experimental import pallas as pl
from jax.experimental.pallas import tpu as pltpu

F32 = jnp.float32
BF16 = jnp.bfloat16
HIGHEST = lax.Precision.HIGHEST

LN_EPS = 1e-5
RMS_EPS = 1e-5
SSD_CHUNK = 128
N_MOD = 6

V7X_VMEM_LIMIT_BYTES = 56 * 1024 * 1024
LANE = 128
SUBLANE = 8
BF16_ROWS = 2 * SUBLANE
FIX_ROWS = BF16_ROWS


def _pick(dim, pref, align):
    best = None
    t = align
    while t <= min(dim, pref):
        if dim % t == 0:
            best = t
        t += align
    return best if best is not None else dim


def _params(sem):
    return pltpu.CompilerParams(dimension_semantics=sem, vmem_limit_bytes=V7X_VMEM_LIMIT_BYTES)


def _silu(x):
    return x * jax.nn.sigmoid(x)


def _split3(v):
    hi = v.astype(BF16)
    r1 = v - hi.astype(F32)
    mid = r1.astype(BF16)
    lo = (r1 - mid.astype(F32)).astype(BF16)
    return hi, mid, lo


def _dot_exact_lhs(sel, v):
    return sum(jnp.dot(sel, p, preferred_element_type=F32) for p in _split3(v))


def _dot_exact_rhs(v, sel):
    return sum(jnp.dot(p, sel, preferred_element_type=F32) for p in _split3(v))


def _mm_kernel(*refs, n_w, n_out, nk, epilogue):
    a_ref = refs[0]
    w_refs = refs[1:1 + n_w]
    o_refs = refs[1 + n_w:1 + n_w + n_out]
    acc_refs = refs[1 + n_w + n_out:]

    def finish(accs):
        if epilogue == "plain":
            o_refs[0][...] = accs[0].astype(o_refs[0].dtype)
        elif epilogue == "swiglu":
            o_refs[0][...] = (_silu(accs[0]) * accs[1]).astype(o_refs[0].dtype)
        else:
            o_refs[0][...] = accs[0].astype(o_refs[0].dtype)
            o_refs[1][...] = (accs[1] * accs[2]).astype(o_refs[1].dtype)

    a = a_ref[...]
    parts = [jnp.dot(a, w[...], preferred_element_type=F32) for w in w_refs]
    if nk == 1:
        finish(parts)
        return
    k = pl.program_id(2)

    @pl.when(k == 0)
    def _():
        for acc, p in zip(acc_refs, parts):
            acc[...] = p

    @pl.when(k > 0)
    def _():
        for acc, p in zip(acc_refs, parts):
            acc[...] += p

    @pl.when(k == nk - 1)
    def _():
        finish([acc[...] for acc in acc_refs])


def _matmul(a, ws, layer, col_offsets, n, out_dtypes, *, epilogue, tm, tn, tk, name, ragged_n=False):
    m, kdim = a.shape
    tm = _pick(m, tm, BF16_ROWS)
    tn = min(tn, n) if ragged_n else _pick(n, tn, LANE)
    tk = _pick(kdim, tk, LANE)
    nk = kdim // tk
    for off in col_offsets:
        assert off % tn == 0
    n_w = len(ws)
    n_out = len(out_dtypes)
    in_specs = [pl.BlockSpec((tm, tk), lambda i, j, k: (i, k))]
    for off in col_offsets:
        in_specs.append(pl.BlockSpec((None, tk, tn),
                                     functools.partial(lambda i, j, k, o: (layer, k, j + o), o=off // tn)))
    out_specs = [pl.BlockSpec((tm, tn), lambda i, j, k: (i, j)) for _ in out_dtypes]
    scratch = [pltpu.VMEM((tm, tn), F32) for _ in ws] if nk > 1 else []
    outs = pl.pallas_call(
        functools.partial(_mm_kernel, n_w=n_w, n_out=n_out, nk=nk, epilogue=epilogue),
        grid=(m // tm, pl.cdiv(n, tn), nk),
        in_specs=in_specs,
        out_specs=out_specs,
        out_shape=[jax.ShapeDtypeStruct((m, n), dt) for dt in out_dtypes],
        scratch_shapes=scratch,
        compiler_params=_params(("parallel", "parallel", "arbitrary")),
        name=name,
    )(a, *ws)
    return outs


def _ada_kernel(c_ref, w_ref, b_ref, o_ref):
    w = w_ref[...].astype(BF16)
    o_ref[...] = jnp.dot(c_ref[...], w, preferred_element_type=F32) + b_ref[...]


def _adaln(c_act, w_ada, b_ada):
    depth, d, n = w_ada.shape
    rows = c_act.shape[0]
    tn = _pick(n, 512, LANE)
    return pl.pallas_call(
        _ada_kernel,
        grid=(depth, n // tn),
        in_specs=[
            pl.BlockSpec((rows, d), lambda l, j: (0, 0)),
            pl.BlockSpec((None, d, tn), lambda l, j: (l, 0, j)),
            pl.BlockSpec((None, 1, tn), lambda l, j: (l, 0, j)),
        ],
        out_specs=pl.BlockSpec((None, rows, tn), lambda l, j: (l, 0, j)),
        out_shape=jax.ShapeDtypeStruct((depth, rows, n), F32),
        compiler_params=_params(("parallel", "parallel")),
        name="adaln",
    )(c_act, w_ada, b_ada.reshape(depth, 1, n))


class _Slab:
    def __init__(self, batch, seq, dec_batch, dec_seq, d):
        self.batch, self.seq, self.dec_batch, self.dec_seq, self.d = batch, seq, dec_batch, dec_seq, d
        self.m_p = batch * seq
        self.m_s = dec_batch * dec_seq
        self.m = self.m_p + self.m_s
        align = math.lcm(dec_seq * SUBLANE, BF16_ROWS)
        self.tm = _pick(math.gcd(seq, self.m_s), 128, align)
        self.n_p = self.m_p // self.tm
        self.n_t = self.m // self.tm
        self.tiles_per_seq = seq // self.tm
        self.tm_conv = _pick(math.gcd(seq, self.m_s), 512, align)


def _mod_specs(g, layer, chunk):
    tm, d = g.tm, g.d
    p_spec = pl.BlockSpec((None, 1, d),
                          lambda i: (jnp.minimum(i // g.tiles_per_seq, g.batch - 1), 0, chunk))
    s_spec = pl.BlockSpec((None, tm // g.dec_seq, d), lambda i: (layer, jnp.maximum(i - g.n_p, 0), chunk))
    return p_spec, s_spec


def _seq_to_tokens(v, tm, dec_seq):
    n_seq = tm // dec_seq
    r = lax.broadcasted_iota(jnp.int32, (tm, n_seq), 0) // dec_seq
    c = lax.broadcasted_iota(jnp.int32, (tm, n_seq), 1)
    sel = jnp.where(r == c, 1.0, 0.0).astype(BF16)
    return _dot_exact_lhs(sel, v)


def _modulate_kernel(x_ref, scp_ref, shp_ref, scs_ref, shs_ref, h_ref, *, n_p, dec_seq):
    i = pl.program_id(0)
    tm = x_ref.shape[0]

    @pl.when(i < n_p)
    def _():
        h_ref[...] = (x_ref[...] * (1.0 + scp_ref[...]) + shp_ref[...]).astype(h_ref.dtype)

    @pl.when(i >= n_p)
    def _():
        sc = _seq_to_tokens(scs_ref[...], tm, dec_seq)
        sh = _seq_to_tokens(shs_ref[...], tm, dec_seq)
        h_ref[...] = (x_ref[...] * (1.0 + sc) + sh).astype(h_ref.dtype)


def _ln_kernel(*refs, n_p, dec_seq, alpha, with_next):
    if with_next:
        (x_ref, o_ref, gp_ref, gs_ref, lg_ref, lb_ref, scp_ref, shp_ref, scs_ref, shs_ref,
         xo_ref, h_ref) = refs
    else:
        x_ref, o_ref, gp_ref, gs_ref, lg_ref, lb_ref, xo_ref = refs
    i = pl.program_id(0)
    tm = x_ref.shape[0]

    def body(g, sc, sh):
        v = alpha * x_ref[...] + g * o_ref[...]
        mu = jnp.mean(v, axis=-1, keepdims=True)
        vc = v - mu
        var = jnp.mean(vc * vc, axis=-1, keepdims=True)
        xn = vc * lax.rsqrt(var + LN_EPS) * lg_ref[...] + lb_ref[...]
        xo_ref[...] = xn
        if with_next:
            h_ref[...] = (xn * (1.0 + sc) + sh).astype(h_ref.dtype)

    @pl.when(i < n_p)
    def _():
        body(gp_ref[...], scp_ref[...] if with_next else None, shp_ref[...] if with_next else None)

    @pl.when(i >= n_p)
    def _():
        body(_seq_to_tokens(gs_ref[...], tm, dec_seq),
             _seq_to_tokens(scs_ref[...], tm, dec_seq) if with_next else None,
             _seq_to_tokens(shs_ref[...], tm, dec_seq) if with_next else None)


def _modulate(g, x, mod, mod_p, layer, chunk_sc, chunk_sh):
    tm, d = g.tm, g.d
    scp, scs = _mod_specs(g, layer, chunk_sc)
    shp, shs = _mod_specs(g, layer, chunk_sh)
    row = pl.BlockSpec((tm, d), lambda i: (i, 0))
    return pl.pallas_call(
        functools.partial(_modulate_kernel, n_p=g.n_p, dec_seq=g.dec_seq),
        grid=(g.n_t,),
        in_specs=[row, scp, shp, scs, shs],
        out_specs=row,
        out_shape=jax.ShapeDtypeStruct((g.m, d), BF16),
        compiler_params=_params(("parallel",)),
        name="modulate",
    )(x, mod_p[layer], mod_p[layer], mod, mod)


def _ln_mod(g, x, o, mod, mod_p, layer, chunk_g, ln_g, ln_b, alpha, nxt):
    tm, d = g.tm, g.d
    row = pl.BlockSpec((tm, d), lambda i: (i, 0))
    vec = pl.BlockSpec((1, d), lambda i: (0, 0))
    gp, gs = _mod_specs(g, layer, chunk_g)
    in_specs = [row, row, gp, gs, vec, vec]
    args = [x, o, mod_p[layer], mod, ln_g.reshape(1, d), ln_b.reshape(1, d)]
    out_specs = [row]
    out_shape = [jax.ShapeDtypeStruct((g.m, d), F32)]
    if nxt is not None:
        n_layer, c_sc, c_sh = nxt
        scp, scs = _mod_specs(g, n_layer, c_sc)
        shp, shs = _mod_specs(g, n_layer, c_sh)
        in_specs += [scp, shp, scs, shs]
        args += [mod_p[n_layer], mod_p[n_layer], mod, mod]
        out_specs.append(row)
        out_shape.append(jax.ShapeDtypeStruct((g.m, d), BF16))
    outs = pl.pallas_call(
        functools.partial(_ln_kernel, n_p=g.n_p, dec_seq=g.dec_seq, alpha=alpha, with_next=nxt is not None),
        grid=(g.n_t,),
        in_specs=in_specs,
        out_specs=out_specs,
        out_shape=out_shape,
        compiler_params=_params(("parallel",)),
        name="ln_mod",
    )(*args)
    return outs if nxt is not None else (outs[0], None)


def _conv_kernel(*refs, n_p, tiles_per_seq, width, dec_seq, with_bias_silu, with_gate):
    u_ref, halo_ref = refs[0], refs[1]
    e_refs = refs[2:2 + width - 1]
    pos = 2 + width - 1
    w_ref = refs[pos]
    pos += 1
    b_ref = g_ref = None
    if with_bias_silu:
        b_ref = refs[pos]
        pos += 1
    if with_gate:
        g_ref = refs[pos]
        pos += 1
    o_ref = refs[pos]
    i = pl.program_id(0)
    w = w_ref[...]

    def post(y, rows):
        if with_bias_silu:
            y = _silu(y + b_ref[...])
        if with_gate:
            y = y * g_ref[rows, :].astype(F32)
        return y.astype(o_ref.dtype)

    u = u_ref[...]

    @pl.when(i < n_p)
    def _():
        y = u * w[width - 1:width, :]
        for s in range(1, width):
            y = y + pltpu.roll(u, s, 0) * w[width - 1 - s:width - s, :]
        o_ref[...] = post(y, slice(None))
        keep = jnp.where(i % tiles_per_seq == 0, 0.0, 1.0)
        halo = halo_ref[...] * keep
        halo = jnp.concatenate([halo, halo], axis=0)
        top = u[0:FIX_ROWS, :]
        r = lax.broadcasted_iota(jnp.int32, top.shape, 0)
        yt = top * w[width - 1:width, :]
        for s in range(1, width):
            sh = jnp.where(r < s, pltpu.roll(halo, s, 0), pltpu.roll(top, s, 0))
            yt = yt + sh * w[width - 1 - s:width - s, :]
        o_ref[0:FIX_ROWS, :] = post(yt, slice(0, FIX_ROWS))

    @pl.when(i >= n_p)
    def _():
        t = lax.broadcasted_iota(jnp.int32, u.shape, 0) % dec_seq
        y = u * w[width - 1:width, :]
        for s in range(1, width):
            sh = jnp.where(t >= s, pltpu.roll(u, s, 0), e_refs[s - 1][...])
            y = y + sh * w[width - 1 - s:width - s, :]
        o_ref[...] = post(y, slice(None))


def _state_rows(buf, s, dec_seq):
    n_seq, km1, c = buf.shape
    assert dec_seq >= km1
    rows = buf[:, km1 - s:, :]
    rows = jnp.pad(rows, ((0, 0), (0, dec_seq - s), (0, 0)))
    return rows.reshape(n_seq * dec_seq, c)


def _causal_conv(g, u, buf_s, w, bias, gate, out_dtype, name):
    m, c = u.shape
    width = w.shape[0]
    tm = g.tm_conv
    tc = _pick(c, 1024, LANE)
    n_p = g.m_p // tm
    es = [_state_rows(buf_s, s, g.dec_seq) for s in range(1, width)]
    tile = pl.BlockSpec((tm, tc), lambda i, j: (i, j))
    halo = pl.BlockSpec((SUBLANE, tc), lambda i, j: (jnp.maximum(i * (tm // SUBLANE) - 1, 0), j))
    e_spec = pl.BlockSpec((tm, tc), lambda i, j: (jnp.maximum(i - n_p, 0), j))
    in_specs = [tile, halo] + [e_spec] * (width - 1) + [pl.BlockSpec((width, tc), lambda i, j: (0, j))]
    args = [u, u] + es + [w]
    if bias is not None:
        in_specs.append(pl.BlockSpec((1, tc), lambda i, j: (0, j)))
        args.append(bias.reshape(1, c))
    if gate is not None:
        in_specs.append(tile)
        args.append(gate)
    return pl.pallas_call(
        functools.partial(_conv_kernel, n_p=n_p, tiles_per_seq=g.seq // tm, width=width,
                          dec_seq=g.dec_seq, with_bias_silu=bias is not None, with_gate=gate is not None),
        grid=(m // tm, c // tc),
        in_specs=in_specs,
        out_specs=tile,
        out_shape=jax.ShapeDtypeStruct((m, c), out_dtype),
        compiler_params=_params(("parallel", "parallel")),
        name=name,
    )(*args)


def _dt_kernel(r_ref, b_ref, al_ref, dt_ref, adt_ref):
    dt = jax.nn.softplus(r_ref[...] + b_ref[...])
    dt_ref[...] = dt
    adt_ref[...] = dt * (-jnp.exp(al_ref[...]))


def _dt_prep(dt_raw, dt_bias, a_log):
    m, h = dt_raw.shape
    tm = _pick(m, 1024, SUBLANE)
    row = pl.BlockSpec((tm, h), lambda i: (i, 0))
    vec = pl.BlockSpec((1, h), lambda i: (0, 0))
    return pl.pallas_call(
        _dt_kernel,
        grid=(m // tm,),
        in_specs=[row, vec, vec],
        out_specs=[row, row],
        out_shape=[jax.ShapeDtypeStruct((m, h), F32)] * 2,
        compiler_params=_params(("parallel",)),
        name="ssd_dt",
    )(dt_raw, dt_bias.reshape(1, h), a_log.reshape(1, h))


def _ssd_one(xs, bm, cm, z, dt_c, adt_c, adt_r, s_old, dsk, ng, expand, expand_t, *, q, r_heads, p_dim):
    n_state = s_old.shape[-1]
    bm = bm.astype(BF16)
    cm = cm.astype(BF16)
    row = lax.broadcasted_iota(jnp.int32, (q, q), 0)
    col = lax.broadcasted_iota(jnp.int32, (q, q), 1)
    causal = row >= col
    tril = jnp.where(causal, 1.0, 0.0).astype(F32)
    triu = jnp.where(row <= col, 1.0, 0.0).astype(F32)
    cs_c = jnp.dot(tril, adt_c, precision=HIGHEST, preferred_element_type=F32)
    cs_r = jnp.dot(adt_r, triu, precision=HIGHEST, preferred_element_type=F32)
    cs_last_row = cs_c[q - 1:q, :]
    cs_last_col = cs_r[:, q - 1:q]

    stacked = jnp.concatenate([dt_c, jnp.exp(cs_c), jnp.exp(cs_last_row - cs_c)], axis=0)
    wide = _dot_exact_rhs(stacked, expand)
    dt_x, ecs_x, dec_x = wide[0:q], wide[q:2 * q], wide[2 * q:3 * q]
    elast = _dot_exact_lhs(expand_t, jnp.broadcast_to(jnp.exp(cs_last_col), (r_heads, n_state)))

    xdt = xs * dt_x
    xb = xdt.astype(BF16)
    y = lax.dot_general(cm, s_old.astype(BF16), (((1,), (1,)), ((), ())),
                        preferred_element_type=F32) * ecs_x
    s_new = s_old * elast + lax.dot_general((xdt * dec_x).astype(BF16), bm, (((0,), (0,)), ((), ())),
                                            preferred_element_type=F32)

    cb = lax.dot_general(cm, bm, (((1,), (1,)), ((), ())), preferred_element_type=F32)
    pieces = []
    heads_per_blk = max(1, LANE // p_dim)
    assert r_heads % heads_per_blk == 0
    for blk in range(r_heads // heads_per_blk):
        width = heads_per_blk * p_dim
        xblk = xb[:, blk * width:(blk + 1) * width]
        lane = lax.broadcasted_iota(jnp.int32, (q, width), 1)
        acc = None
        for k in range(heads_per_blk):
            h = blk * heads_per_blk + k
            diff = cs_c[:, h:h + 1] - cs_r[h:h + 1, :]
            lmat = jnp.exp(jnp.where(causal, diff, -jnp.inf))
            yk = jnp.dot((cb * lmat).astype(BF16), xblk, preferred_element_type=F32)
            acc = yk if acc is None else jnp.where(lane >= k * p_dim, yk, acc)
        pieces.append(acc)
    y = y + (pieces[0] if len(pieces) == 1 else jnp.concatenate(pieces, axis=1))

    y = y + dsk * xs
    yz = y * _silu(z)
    ms = jnp.mean(yz * yz, axis=-1, keepdims=True)
    return yz * lax.rsqrt(ms + RMS_EPS) * ng, s_new


def _ssd_kernel(*refs, q, n_sub, r_heads, p_dim, zero_init, n_alias):
    (xs_ref, b_ref, c_ref, z_ref, dtc_ref, adtc_ref, adtr_ref, dsk_ref, ng_ref, ex_ref, ext_ref) = refs[:11]
    pos = 11
    s0_ref = None
    if not zero_init:
        s0_ref = refs[pos]
        pos += 1
    pos += n_alias
    y_ref, s_ref = refs[pos], refs[pos + 1]
    ci = pl.program_id(2)

    @pl.when(ci == 0)
    def _():
        if zero_init:
            s_ref[...] = jnp.zeros_like(s_ref)
        else:
            s_ref[...] = s0_ref[...]

    ys = []
    for k in range(n_sub):
        rows = slice(k * q, (k + 1) * q)
        y, s_new = _ssd_one(xs_ref[rows, :], b_ref[rows, :], c_ref[rows, :], z_ref[rows, :],
                            dtc_ref[k], adtc_ref[k], adtr_ref[k], s_ref[k], dsk_ref[...], ng_ref[...],
                            ex_ref[...], ext_ref[...], q=q, r_heads=r_heads, p_dim=p_dim)
        s_ref[k] = s_new
        ys.append(y)
    y_all = ys[0] if n_sub == 1 else jnp.concatenate(ys, axis=0)
    y_ref[...] = y_all.astype(y_ref.dtype)


def _ssd_scan(xbc, z, dt, adt, d_skip_x, norm_g, s0, layer, y_buf, s_buf, *, m, n_layers, row0, n_seq,
              seq_len, n_sub, groups, r_heads, p_dim, n_state):
    q = SSD_CHUNK if seq_len % SSD_CHUNK == 0 else seq_len
    nc = seq_len // q
    rp = r_heads * p_dim
    d_inner = groups * rp
    rows_blk = n_sub * q
    assert nc == 1 or n_sub == 1
    assert row0 % rows_blk == 0 and n_seq % n_sub == 0 and rows_blk % BF16_ROWS == 0
    assert rp % LANE == 0 and n_state % LANE == 0 and d_inner % n_state == 0
    blk0 = row0 // rows_blk
    rows = n_seq * seq_len

    def per_group(v):
        v = v[row0:row0 + rows].reshape(n_seq * nc, q, groups, r_heads)
        return v.transpose(2, 0, 1, 3), v.transpose(2, 0, 3, 1)

    dt_c, _ = per_group(dt)
    adt_c, adt_r = per_group(adt)
    head_of = np.arange(rp) // p_dim
    expand = jnp.asarray(head_of[None, :] == np.arange(r_heads)[:, None], dtype=BF16)
    expand_t = jnp.asarray(head_of[:, None] == np.arange(r_heads)[None, :], dtype=BF16)

    b_col0 = d_inner // n_state
    tok = lambda s, g, c: (blk0 + s * nc + c, g)
    per_chunk = lambda s, g, c: (g, s * nc + c, 0, 0)
    const = lambda s, g, c: (0, 0)
    in_specs = [
        pl.BlockSpec((rows_blk, rp), tok),
        pl.BlockSpec((rows_blk, n_state), lambda s, g, c: (blk0 + s * nc + c, b_col0 + g)),
        pl.BlockSpec((rows_blk, n_state), lambda s, g, c: (blk0 + s * nc + c, b_col0 + groups + g)),
        pl.BlockSpec((rows_blk, rp), tok),
        pl.BlockSpec((None, n_sub, q, r_heads), per_chunk),
        pl.BlockSpec((None, n_sub, q, r_heads), per_chunk),
        pl.BlockSpec((None, n_sub, r_heads, q), per_chunk),
        pl.BlockSpec((1, rp), lambda s, g, c: (0, g)),
        pl.BlockSpec((1, rp), lambda s, g, c: (0, g)),
        pl.BlockSpec((r_heads, rp), const),
        pl.BlockSpec((rp, r_heads), const),
    ]
    args = [xbc, xbc, xbc, z, dt_c, adt_c, adt_r, d_skip_x, norm_g.reshape(1, d_inner), expand, expand_t]
    state_spec = pl.BlockSpec((None, n_sub, None, rp, n_state), lambda s, g, c: (layer, s, g, 0, 0))
    if s0 is not None:
        in_specs.append(state_spec)
        args.append(s0.reshape(n_layers, n_seq, groups, rp, n_state))
    aliases = {}
    for out_idx, buf in enumerate((y_buf, s_buf)):
        if buf is not None:
            aliases[len(args)] = out_idx
            in_specs.append(pl.BlockSpec(memory_space=pl.ANY))
            args.append(buf)
    y, s_new = pl.pallas_call(
        functools.partial(_ssd_kernel, q=q, n_sub=n_sub, r_heads=r_heads, p_dim=p_dim,
                          zero_init=s0 is None, n_alias=len(aliases)),
        grid=(n_seq // n_sub, groups, nc),
        in_specs=in_specs,
        out_specs=[pl.BlockSpec((rows_blk, rp), tok), state_spec],
        out_shape=[jax.ShapeDtypeStruct((m, d_inner), BF16),
                   jax.ShapeDtypeStruct((n_layers, n_seq, groups, rp, n_state), F32)],
        input_output_aliases=aliases,
        compiler_params=_params(("parallel", "parallel", "arbitrary")),
        name="ssd_scan_q%d" % q,
    )(*args)
    return y, s_new


def _tails(u, g, k):
    c = u.shape[1]
    last = np.arange(-k, 0)
    rows_p = ((np.arange(g.batch)[:, None] + 1) * g.seq + last[None, :]).reshape(-1)
    rows_s = (g.m_p + (np.arange(g.dec_batch)[:, None] + 1) * g.dec_seq + last[None, :]).reshape(-1)
    p = jnp.take(u, jnp.asarray(rows_p, jnp.int32), axis=0).reshape(g.batch, k, c)
    s = jnp.take(u, jnp.asarray(rows_s, jnp.int32), axis=0).reshape(g.dec_batch, k, c)
    return p, s


def kernel(x_prompt, x_sample, state_sconv, state_ssd_conv, state_ssd, c_prompt, c_sample, w_ada, b_ada, ln_g, ln_b, sc_w_in, sc_conv_w, sc_w_out, ssd_w_in, ssd_conv_w, ssd_conv_b, ssd_dt_bias, ssd_a_log, ssd_d, ssd_norm_g, ssd_w_out, ffn_w_gate, ffn_w_up, ffn_w_down):
    batch, seq, d = x_prompt.shape
    dec_batch, dec_seq, _ = x_sample.shape
    depth = w_ada.shape[0]
    g = _Slab(batch, seq, dec_batch, dec_seq, d)
    alpha = (2 * depth) ** 0.25

    n_ssd, _, heads, p_dim, n_state = state_ssd.shape
    d_inner = heads * p_dim
    conv_dim = ssd_conv_w.shape[-1]
    groups = (conv_dim - d_inner) // (2 * n_state)
    r_heads = heads // groups
    ffn_hidden = ffn_w_gate.shape[-1]

    n_c = batch + dec_batch
    n_c_pad = -(-n_c // BF16_ROWS) * BF16_ROWS
    c_all = jnp.concatenate([c_sample, c_prompt], axis=0)
    c_act = jnp.pad(_silu(c_all), ((0, n_c_pad - n_c), (0, 0))).astype(BF16)
    mod = _adaln(c_act, w_ada, b_ada)
    mod_p = [mod[l, dec_batch:n_c].reshape(batch, 1, N_MOD * d) for l in range(depth)]

    sc_w_in_b, sc_w_out_b = sc_w_in.astype(BF16), sc_w_out.astype(BF16)
    ssd_w_in_b, ssd_w_out_b = ssd_w_in.astype(BF16), ssd_w_out.astype(BF16)
    w_gate_b, w_up_b, w_down_b = ffn_w_gate.astype(BF16), ffn_w_up.astype(BF16), ffn_w_down.astype(BF16)

    x = jnp.concatenate([x_prompt.reshape(g.m_p, d), x_sample.reshape(g.m_s, d)], axis=0)
    h = _modulate(g, x, mod, mod_p, 0, 1, 0)

    sc_p, sc_s, cv_p, cv_s = [], [], [], []
    st_p = st_s = None
    for i in range(depth):
        j = i // 2
        if i % 2 == 0:
            gate_b, u = _matmul(h, [sc_w_in_b] * 3, j, [0, d, 2 * d], d, [BF16, F32],
                                epilogue="sconv", tm=1024, tn=512, tk=d, name="sconv_in")
            y = _causal_conv(g, u, state_sconv[j], sc_conv_w[j], None, gate_b, BF16, "sconv_conv")
            tp, ts = _tails(u, g, sc_conv_w.shape[1] - 1)
            sc_p.append(tp)
            sc_s.append(ts)
            (out,) = _matmul(y, [sc_w_out_b], j, [0], d, [F32],
                             epilogue="plain", tm=1024, tn=1024, tk=4096, name="sconv_out")
        else:
            (z,) = _matmul(h, [ssd_w_in_b], j, [0], d_inner, [F32],
                           epilogue="plain", tm=1024, tn=1024, tk=4096, name="ssd_in_z")
            (xbc,) = _matmul(h, [ssd_w_in_b], j, [d_inner], conv_dim, [F32],
                             epilogue="plain", tm=1024, tn=1024, tk=4096, name="ssd_in_xbc")
            (dt_raw,) = _matmul(h, [ssd_w_in_b], j, [d_inner + conv_dim], heads, [F32],
                                epilogue="plain", tm=1024, tn=heads, tk=4096, name="ssd_in_dt")
            tp, ts = _tails(xbc, g, ssd_conv_w.shape[1] - 1)
            cv_p.append(tp)
            cv_s.append(ts)
            xbc_act = _causal_conv(g, xbc, state_ssd_conv[j], ssd_conv_w[j], ssd_conv_b[j], None, F32, "ssd_conv")
            dt, adt = _dt_prep(dt_raw, ssd_dt_bias[j], ssd_a_log[j])
            d_skip_x = jnp.repeat(ssd_d[j], p_dim).reshape(1, d_inner)
            common = dict(m=g.m, n_layers=n_ssd, groups=groups, r_heads=r_heads, p_dim=p_dim, n_state=n_state)
            y, st_p = _ssd_scan(xbc_act, z, dt, adt, d_skip_x, ssd_norm_g[j], None, j, None, st_p,
                                row0=0, n_seq=batch, seq_len=seq, n_sub=1, **common)
            n_sub = _pick(dec_batch, 4, 1) if dec_seq % SSD_CHUNK else 1
            while (n_sub * dec_seq) % BF16_ROWS:
                n_sub *= 2
            y, st_s = _ssd_scan(xbc_act, z, dt, adt, d_skip_x, ssd_norm_g[j], state_ssd, j, y, st_s,
                                row0=g.m_p, n_seq=dec_batch, seq_len=dec_seq, n_sub=n_sub, **common)
            (out,) = _matmul(y, [ssd_w_out_b], j, [0], d, [F32],
                             epilogue="plain", tm=1024, tn=1024, tk=4096, name="ssd_out")
        x, h = _ln_mod(g, x, out, mod, mod_p, i, 2, ln_g[i, 0], ln_b[i, 0], alpha, (i, 4, 3))

        (hid,) = _matmul(h, [w_gate_b, w_up_b], i, [0, 0], ffn_hidden, [BF16],
                         epilogue="swiglu", tm=1024, tn=512, tk=d, name="ffn_in", ragged_n=True)
        (out,) = _matmul(hid, [w_down_b], i, [0], d, [F32],
                         epilogue="plain", tm=1024, tn=512, tk=5632, name="ffn_out")
        nxt = (i + 1, 1, 0) if i + 1 < depth else None
        x, h = _ln_mod(g, x, out, mod, mod_p, i, 5, ln_g[i, 1], ln_b[i, 1], alpha, nxt)

    y_prompt = x[:g.m_p].reshape(batch, seq, d)
    y_sample = x[g.m_p:].reshape(dec_batch, dec_seq, d)
    st_p = st_p.reshape(n_ssd, batch, heads, p_dim, n_state)
    st_s = st_s.reshape(n_ssd, dec_batch, heads, p_dim, n_state)
    return (y_prompt, y_sample, jnp.stack(sc_p), jnp.stack(cv_p), st_p,
            jnp.stack(sc_s), jnp.stack(cv_s), st_s)
```

```python
import functools
import math

import numpy as np
import jax
import jax.numpy as jnp
from jax import lax
from jax.experimental import pallas as pl
from jax.experimental.pallas import tpu as pltpu

F32 = jnp.float32
BF16 = jnp.bfloat16
HIGHEST = lax.Precision.HIGHEST

LN_EPS = 1e-5
RMS_EPS = 1e-5
SSD_CHUNK = 128
N_MOD = 6

V7X_VMEM_LIMIT_BYTES = 56 * 1024 * 1024
LANE = 128
SUBLANE = 8
BF16_ROWS = 2 * SUBLANE
FIX_ROWS = BF16_ROWS
CONV_TILE_ROWS = 256
CONV_TILE_COLS = 5120


def _pick(dim, pref, align):
    best = None
    t = align
    while t <= min(dim, pref):
        if dim % t == 0:
            best = t
        t += align
    return best if best is not None else dim


def _params(sem):
    return pltpu.CompilerParams(dimension_semantics=sem, vmem_limit_bytes=V7X_VMEM_LIMIT_BYTES)


def _silu(x):
    return x * jax.nn.sigmoid(x)


def _split3(v):
    hi = v.astype(BF16)
    r1 = v - hi.astype(F32)
    mid = r1.astype(BF16)
    lo = (r1 - mid.astype(F32)).astype(BF16)
    return hi, mid, lo


def _dot_exact_lhs(sel, v):
    return sum(jnp.dot(sel, p, preferred_element_type=F32) for p in _split3(v))


def _dot_exact_rhs(v, sel):
    return sum(jnp.dot(p, sel, preferred_element_type=F32) for p in _split3(v))


def _mm_kernel(*refs, n_w, n_out, nk, epilogue):
    a_ref = refs[0]
    w_refs = refs[1:1 + n_w]
    o_refs = refs[1 + n_w:1 + n_w + n_out]
    acc_refs = refs[1 + n_w + n_out:]

    def finish(accs):
        if epilogue == "plain":
            o_refs[0][...] = accs[0].astype(o_refs[0].dtype)
        elif epilogue == "swiglu":
            o_refs[0][...] = (_silu(accs[0]) * accs[1]).astype(o_refs[0].dtype)
        else:
            o_refs[0][...] = accs[0].astype(o_refs[0].dtype)
            o_refs[1][...] = (accs[1] * accs[2]).astype(o_refs[1].dtype)

    a = a_ref[...]
    parts = [jnp.dot(a, w[...], preferred_element_type=F32) for w in w_refs]
    if nk == 1:
        finish(parts)
        return
    k = pl.program_id(2)

    @pl.when(k == 0)
    def _():
        for acc, p in zip(acc_refs, parts):
            acc[...] = p

    @pl.when(k > 0)
    def _():
        for acc, p in zip(acc_refs, parts):
            acc[...] += p

    @pl.when(k == nk - 1)
    def _():
        finish([acc[...] for acc in acc_refs])


def _matmul(a, ws, layer, col_offsets, n, out_dtypes, *, epilogue, tm, tn, tk, name, ragged_n=False):
    m, kdim = a.shape
    tm = _pick(m, tm, BF16_ROWS)
    tn = min(tn, n) if ragged_n else _pick(n, tn, LANE)
    tk = _pick(kdim, tk, LANE)
    nk = kdim // tk
    for off in col_offsets:
        assert off % tn == 0
    n_w = len(ws)
    n_out = len(out_dtypes)
    in_specs = [pl.BlockSpec((tm, tk), lambda i, j, k: (i, k))]
    for off in col_offsets:
        in_specs.append(pl.BlockSpec((None, tk, tn),
                                     functools.partial(lambda i, j, k, o: (layer, k, j + o), o=off // tn)))
    out_specs = [pl.BlockSpec((tm, tn), lambda i, j, k: (i, j)) for _ in out_dtypes]
    scratch = [pltpu.VMEM((tm, tn), F32) for _ in ws] if nk > 1 else []
    outs = pl.pallas_call(
        functools.partial(_mm_kernel, n_w=n_w, n_out=n_out, nk=nk, epilogue=epilogue),
        grid=(m // tm, pl.cdiv(n, tn), nk),
        in_specs=in_specs,
        out_specs=out_specs,
        out_shape=[jax.ShapeDtypeStruct((m, n), dt) for dt in out_dtypes],
        scratch_shapes=scratch,
        compiler_params=_params(("parallel", "parallel", "arbitrary")),
        name=name,
    )(a, *ws)
    return outs


def _ada_kernel(c_ref, w_ref, b_ref, o_ref):
    w = w_ref[...].astype(BF16)
    o_ref[...] = jnp.dot(c_ref[...], w, preferred_element_type=F32) + b_ref[...]


def _adaln(c_act, w_ada, b_ada):
    depth, d, n = w_ada.shape
    rows = c_act.shape[0]
    tn = _pick(n, 512, LANE)
    return pl.pallas_call(
        _ada_kernel,
        grid=(depth, n // tn),
        in_specs=[
            pl.BlockSpec((rows, d), lambda l, j: (0, 0)),
            pl.BlockSpec((None, d, tn), lambda l, j: (l, 0, j)),
            pl.BlockSpec((None, 1, tn), lambda l, j: (l, 0, j)),
        ],
        out_specs=pl.BlockSpec((None, rows, tn), lambda l, j: (l, 0, j)),
        out_shape=jax.ShapeDtypeStruct((depth, rows, n), F32),
        compiler_params=_params(("parallel", "parallel")),
        name="adaln",
    )(c_act, w_ada, b_ada.reshape(depth, 1, n))


class _Slab:
    def __init__(self, batch, seq, dec_batch, dec_seq, d):
        self.batch, self.seq, self.dec_batch, self.dec_seq, self.d = batch, seq, dec_batch, dec_seq, d
        self.m_p = batch * seq
        self.m_s = dec_batch * dec_seq
        self.m = self.m_p + self.m_s
        align = math.lcm(dec_seq * SUBLANE, BF16_ROWS)
        self.tm = _pick(math.gcd(seq, self.m_s), 128, align)
        self.n_p = self.m_p // self.tm
        self.n_t = self.m // self.tm
        self.tiles_per_seq = seq // self.tm
        self.tm_conv = _pick(math.gcd(seq, self.m_s), CONV_TILE_ROWS, align)


def _mod_specs(g, layer, chunk):
    tm, d = g.tm, g.d
    p_spec = pl.BlockSpec((None, 1, d),
                          lambda i: (jnp.minimum(i // g.tiles_per_seq, g.batch - 1), 0, chunk))
    s_spec = pl.BlockSpec((None, tm // g.dec_seq, d), lambda i: (layer, jnp.maximum(i - g.n_p, 0), chunk))
    return p_spec, s_spec


def _seq_to_tokens(v, tm, dec_seq):
    n_seq = tm // dec_seq
    r = lax.broadcasted_iota(jnp.int32, (tm, n_seq), 0) // dec_seq
    c = lax.broadcasted_iota(jnp.int32, (tm, n_seq), 1)
    sel = jnp.where(r == c, 1.0, 0.0).astype(BF16)
    return _dot_exact_lhs(sel, v)


def _modulate_kernel(x_ref, scp_ref, shp_ref, scs_ref, shs_ref, h_ref, *, n_p, dec_seq):
    i = pl.program_id(0)
    tm = x_ref.shape[0]

    @pl.when(i < n_p)
    def _():
        h_ref[...] = (x_ref[...] * (1.0 + scp_ref[...]) + shp_ref[...]).astype(h_ref.dtype)

    @pl.when(i >= n_p)
    def _():
        sc = _seq_to_tokens(scs_ref[...], tm, dec_seq)
        sh = _seq_to_tokens(shs_ref[...], tm, dec_seq)
        h_ref[...] = (x_ref[...] * (1.0 + sc) + sh).astype(h_ref.dtype)


def _ln_kernel(*refs, n_p, dec_seq, alpha, with_next):
    if with_next:
        (x_ref, o_ref, gp_ref, gs_ref, lg_ref, lb_ref, scp_ref, shp_ref, scs_ref, shs_ref,
         xo_ref, h_ref) = refs
    else:
        x_ref, o_ref, gp_ref, gs_ref, lg_ref, lb_ref, xo_ref = refs
    i = pl.program_id(0)
    tm = x_ref.shape[0]

    def body(g, sc, sh):
        v = alpha * x_ref[...] + g * o_ref[...]
        mu = jnp.mean(v, axis=-1, keepdims=True)
        vc = v - mu
        var = jnp.mean(vc * vc, axis=-1, keepdims=True)
        xn = vc * lax.rsqrt(var + LN_EPS) * lg_ref[...] + lb_ref[...]
        xo_ref[...] = xn
        if with_next:
            h_ref[...] = (xn * (1.0 + sc) + sh).astype(h_ref.dtype)

    @pl.when(i < n_p)
    def _():
        body(gp_ref[...], scp_ref[...] if with_next else None, shp_ref[...] if with_next else None)

    @pl.when(i >= n_p)
    def _():
        body(_seq_to_tokens(gs_ref[...], tm, dec_seq),
             _seq_to_tokens(scs_ref[...], tm, dec_seq) if with_next else None,
             _seq_to_tokens(shs_ref[...], tm, dec_seq) if with_next else None)


def _modulate(g, x, mod, mod_p, layer, chunk_sc, chunk_sh):
    tm, d = g.tm, g.d
    scp, scs = _mod_specs(g, layer, chunk_sc)
    shp, shs = _mod_specs(g, layer, chunk_sh)
    row = pl.BlockSpec((tm, d), lambda i: (i, 0))
    return pl.pallas_call(
        functools.partial(_modulate_kernel, n_p=g.n_p, dec_seq=g.dec_seq),
        grid=(g.n_t,),
        in_specs=[row, scp, shp, scs, shs],
        out_specs=row,
        out_shape=jax.ShapeDtypeStruct((g.m, d), BF16),
        compiler_params=_params(("parallel",)),
        name="modulate",
    )(x, mod_p[layer], mod_p[layer], mod, mod)


def _ln_mod(g, x, o, mod, mod_p, layer, chunk_g, ln_g, ln_b, alpha, nxt):
    tm, d = g.tm, g.d
    row = pl.BlockSpec((tm, d), lambda i: (i, 0))
    vec = pl.BlockSpec((1, d), lambda i: (0, 0))
    gp, gs = _mod_specs(g, layer, chunk_g)
    in_specs = [row, row, gp, gs, vec, vec]
    args = [x, o, mod_p[layer], mod, ln_g.reshape(1, d), ln_b.reshape(1, d)]
    out_specs = [row]
    out_shape = [jax.ShapeDtypeStruct((g.m, d), F32)]
    if nxt is not None:
        n_layer, c_sc, c_sh = nxt
        scp, scs = _mod_specs(g, n_layer, c_sc)
        shp, shs = _mod_specs(g, n_layer, c_sh)
        in_specs += [scp, shp, scs, shs]
        args += [mod_p[n_layer], mod_p[n_layer], mod, mod]
        out_specs.append(row)
        out_shape.append(jax.ShapeDtypeStruct((g.m, d), BF16))
    outs = pl.pallas_call(
        functools.partial(_ln_kernel, n_p=g.n_p, dec_seq=g.dec_seq, alpha=alpha, with_next=nxt is not None),
        grid=(g.n_t,),
        in_specs=in_specs,
        out_specs=out_specs,
        out_shape=out_shape,
        compiler_params=_params(("parallel",)),
        name="ln_mod",
    )(*args)
    return outs if nxt is not None else (outs[0], None)


def _conv_kernel(*refs, n_p, tiles_per_seq, width, dec_seq, with_bias_silu, with_gate):
    u_ref, halo_ref, e_ref, w_ref = refs[:4]
    pos = 4
    b_ref = g_ref = None
    if with_bias_silu:
        b_ref = refs[pos]
        pos += 1
    if with_gate:
        g_ref = refs[pos]
        pos += 1
    o_ref = refs[pos]
    i = pl.program_id(0)
    w = w_ref[...]

    def post(y, rows):
        if with_bias_silu:
            y = _silu(y + b_ref[...])
        if with_gate:
            y = y * g_ref[rows, :].astype(F32)
        return y.astype(o_ref.dtype)

    u = u_ref[...]

    @pl.when(i < n_p)
    def _():
        y = u * w[width - 1:width, :]
        for s in range(1, width):
            y = y + pltpu.roll(u, s, 0) * w[width - 1 - s:width - s, :]
        o_ref[...] = post(y, slice(None))
        keep = jnp.where(i % tiles_per_seq == 0, 0.0, 1.0)
        halo = halo_ref[...] * keep
        halo = jnp.concatenate([halo, halo], axis=0)
        top = u[0:FIX_ROWS, :]
        r = lax.broadcasted_iota(jnp.int32, top.shape, 0)
        yt = top * w[width - 1:width, :]
        for s in range(1, width):
            sh = jnp.where(r < s, pltpu.roll(halo, s, 0), pltpu.roll(top, s, 0))
            yt = yt + sh * w[width - 1 - s:width - s, :]
        o_ref[0:FIX_ROWS, :] = post(yt, slice(0, FIX_ROWS))

    @pl.when(i >= n_p)
    def _():
        tm = u.shape[0]
        t = lax.broadcasted_iota(jnp.int32, u.shape, 0) % dec_seq
        e = e_ref[...]
        y = u * w[width - 1:width, :]
        for s in range(1, width):
            back = width - 1 - s
            src = pltpu.roll(e, tm - back, 0) if back else e
            sh = jnp.where(t >= s, pltpu.roll(u, s, 0), src)
            y = y + sh * w[width - 1 - s:width - s, :]
        o_ref[...] = post(y, slice(None))


def _state_rows(buf, dec_seq):
    n_seq, km1, c = buf.shape
    assert dec_seq >= km1
    return jnp.pad(buf, ((0, 0), (0, dec_seq - km1), (0, 0))).reshape(n_seq * dec_seq, c)


def _causal_conv(g, u, buf_s, w, bias, gate, out_dtype, name):
    m, c = u.shape
    width = w.shape[0]
    tm = g.tm_conv
    tc = _pick(c, CONV_TILE_COLS, LANE)
    n_p = g.m_p // tm
    tile = pl.BlockSpec((tm, tc), lambda i, j: (i, j))
    halo = pl.BlockSpec((SUBLANE, tc), lambda i, j: (jnp.maximum(i * (tm // SUBLANE) - 1, 0), j))
    e_spec = pl.BlockSpec((tm, tc), lambda i, j: (jnp.maximum(i - n_p, 0), j))
    in_specs = [tile, halo, e_spec, pl.BlockSpec((width, tc), lambda i, j: (0, j))]
    args = [u, u, _state_rows(buf_s, g.dec_seq), w]
    if bias is not None:
        in_specs.append(pl.BlockSpec((1, tc), lambda i, j: (0, j)))
        args.append(bias.reshape(1, c))
    if gate is not None:
        in_specs.append(tile)
        args.append(gate)
    return pl.pallas_call(
        functools.partial(_conv_kernel, n_p=n_p, tiles_per_seq=g.seq // tm, width=width,
                          dec_seq=g.dec_seq, with_bias_silu=bias is not None, with_gate=gate is not None),
        grid=(m // tm, c // tc),
        in_specs=in_specs,
        out_specs=tile,
        out_shape=jax.ShapeDtypeStruct((m, c), out_dtype),
        compiler_params=_params(("parallel", "parallel")),
        name=name,
    )(*args)


def _dt_kernel(r_ref, b_ref, al_ref, dt_ref, adt_ref):
    dt = jax.nn.softplus(r_ref[...] + b_ref[...])
    dt_ref[...] = dt
    adt_ref[...] = dt * (-jnp.exp(al_ref[...]))


def _dt_prep(dt_raw, dt_bias, a_log):
    m, h = dt_raw.shape
    tm = _pick(m, 1024, SUBLANE)
    row = pl.BlockSpec((tm, h), lambda i: (i, 0))
    vec = pl.BlockSpec((1, h), lambda i: (0, 0))
    return pl.pallas_call(
        _dt_kernel,
        grid=(m // tm,),
        in_specs=[row, vec, vec],
        out_specs=[row, row],
        out_shape=[jax.ShapeDtypeStruct((m, h), F32)] * 2,
        compiler_params=_params(("parallel",)),
        name="ssd_dt",
    )(dt_raw, dt_bias.reshape(1, h), a_log.reshape(1, h))


def _ssd_kernel(*refs, q, n_sub, r_heads, p_dim, zero_init, n_alias):
    (xs_ref, b_ref, c_ref, z_ref, dtc_ref, adtc_ref, adtr_ref, dsk_ref, ng_ref, ex_ref) = refs[:10]
    pos = 10
    s0_ref = None
    if not zero_init:
        s0_ref = refs[pos]
        pos += 1
    pos += n_alias
    y_ref, s_ref = refs[pos], refs[pos + 1]
    ci = pl.program_id(2)

    @pl.when(ci == 0)
    def _():
        if zero_init:
            s_ref[...] = jnp.zeros_like(s_ref)
        else:
            s_ref[...] = s0_ref[...]

    rows = n_sub * q
    n_state = s_ref.shape[-1]
    xs = xs_ref[...]
    b_f = b_ref[...]
    c_f = c_ref[...]
    bm = b_f.astype(BF16)
    cm = c_f.astype(BF16)
    dt_c = dtc_ref[...]
    adt_c = adtc_ref[...]
    adt_r = adtr_ref[...]

    row = lax.broadcasted_iota(jnp.int32, (rows, rows), 0)
    col = lax.broadcasted_iota(jnp.int32, (rows, rows), 1)
    if n_sub == 1:
        causal = row >= col
        anti = row <= col
    else:
        same = (row // q) == (col // q)
        causal = jnp.logical_and(same, row >= col)
        anti = jnp.logical_and(same, row <= col)
    tril = jnp.where(causal, 1.0, 0.0).astype(F32)
    triu = jnp.where(anti, 1.0, 0.0).astype(F32)
    cs_c = jnp.dot(tril, adt_c, precision=HIGHEST, preferred_element_type=F32)
    cs_r = jnp.dot(adt_r, triu, precision=HIGHEST, preferred_element_type=F32)
    lasts = [cs_c[k * q + q - 1:k * q + q, :] for k in range(n_sub)]
    if n_sub == 1:
        cs_last = lasts[0]
    else:
        cs_last = jnp.concatenate([jnp.broadcast_to(v, (q, r_heads)) for v in lasts], axis=0)

    stacked = jnp.concatenate([dt_c, jnp.exp(cs_c), jnp.exp(cs_last - cs_c)], axis=0)
    wide = _dot_exact_rhs(stacked, ex_ref[...])
    dt_x, ecs_x, dec_x = wide[0:rows], wide[rows:2 * rows], wide[2 * rows:3 * rows]

    xdt = xs * dt_x
    xb = xdt.astype(BF16)
    xdec = xdt * dec_x

    y_state = []
    for k in range(n_sub):
        seq = slice(k * q, (k + 1) * q)
        s_old = s_ref[k]
        y_state.append(lax.dot_general(c_f[seq].astype(BF16), s_old.astype(BF16), (((1,), (1,)), ((), ())),
                                       preferred_element_type=F32))
        upd = lax.dot_general(xdec[seq].astype(BF16), b_f[seq].astype(BF16), (((0,), (0,)), ((), ())),
                              preferred_element_type=F32)
        e_heads = jnp.broadcast_to(jnp.exp(cs_r[:, k * q + q - 1:k * q + q]), (r_heads, n_state))
        for h in range(r_heads):
            hp = slice(h * p_dim, (h + 1) * p_dim)
            s_ref[k, hp, :] = s_old[hp] * e_heads[h:h + 1, :] + upd[hp]
    y = (y_state[0] if n_sub == 1 else jnp.concatenate(y_state, axis=0)) * ecs_x

    cb = lax.dot_general(cm, bm, (((1,), (1,)), ((), ())), preferred_element_type=F32)
    pieces = []
    heads_per_blk = max(1, LANE // p_dim)
    assert r_heads % heads_per_blk == 0
    for blk in range(r_heads // heads_per_blk):
        width = heads_per_blk * p_dim
        xblk = xb[:, blk * width:(blk + 1) * width]
        lane = lax.broadcasted_iota(jnp.int32, (rows, width), 1)
        acc = None
        for k in range(heads_per_blk):
            h = blk * heads_per_blk + k
            diff = cs_c[:, h:h + 1] - cs_r[h:h + 1, :]
            lmat = jnp.exp(jnp.where(causal, diff, -jnp.inf))
            yk = jnp.dot((cb * lmat).astype(BF16), xblk, preferred_element_type=F32)
            acc = yk if acc is None else jnp.where(lane >= k * p_dim, yk, acc)
        pieces.append(acc)
    y = y + (pieces[0] if len(pieces) == 1 else jnp.concatenate(pieces, axis=1))

    y = y + dsk_ref[...] * xs
    yz = y * _silu(z_ref[...].astype(F32))
    ms = jnp.mean(yz * yz, axis=-1, keepdims=True)
    y_ref[...] = (yz * lax.rsqrt(ms + RMS_EPS) * ng_ref[...]).astype(y_ref.dtype)


def _ssd_scan(xbc, z, dt, adt, d_skip_x, norm_g, s0, layer, y_buf, s_buf, *, m, n_layers, row0, n_seq,
              seq_len, n_sub, groups, r_heads, p_dim, n_state):
    q = SSD_CHUNK if seq_len % SSD_CHUNK == 0 else seq_len
    nc = seq_len // q
    rp = r_heads * p_dim
    d_inner = groups * rp
    rows_blk = n_sub * q
    assert nc == 1 or n_sub == 1
    assert row0 % rows_blk == 0 and n_seq % n_sub == 0 and rows_blk % BF16_ROWS == 0
    assert rp % LANE == 0 and n_state % LANE == 0 and d_inner % n_state == 0
    blk0 = row0 // rows_blk
    rows = n_seq * seq_len

    def per_group(v):
        v = v[row0:row0 + rows].reshape(rows // rows_blk, rows_blk, groups, r_heads)
        return v.transpose(2, 0, 1, 3), v.transpose(2, 0, 3, 1)

    dt_c, _ = per_group(dt)
    adt_c, adt_r = per_group(adt)
    head_of = np.arange(rp) // p_dim
    expand = jnp.asarray(head_of[None, :] == np.arange(r_heads)[:, None], dtype=BF16)

    b_col0 = d_inner // n_state
    tok = lambda s, g, c: (blk0 + s * nc + c, g)
    per_chunk = lambda s, g, c: (g, s * nc + c, 0, 0)
    const = lambda s, g, c: (0, 0)
    in_specs = [
        pl.BlockSpec((rows_blk, rp), tok),
        pl.BlockSpec((rows_blk, n_state), lambda s, g, c: (blk0 + s * nc + c, b_col0 + g)),
        pl.BlockSpec((rows_blk, n_state), lambda s, g, c: (blk0 + s * nc + c, b_col0 + groups + g)),
        pl.BlockSpec((rows_blk, rp), tok),
        pl.BlockSpec((None, None, rows_blk, r_heads), per_chunk),
        pl.BlockSpec((None, None, rows_blk, r_heads), per_chunk),
        pl.BlockSpec((None, None, r_heads, rows_blk), per_chunk),
        pl.BlockSpec((1, rp), lambda s, g, c: (0, g)),
        pl.BlockSpec((1, rp), lambda s, g, c: (0, g)),
        pl.BlockSpec((r_heads, rp), const),
    ]
    args = [xbc, xbc, xbc, z, dt_c, adt_c, adt_r, d_skip_x, norm_g.reshape(1, d_inner), expand]
    state_spec = pl.BlockSpec((None, n_sub, None, rp, n_state), lambda s, g, c: (layer, s, g, 0, 0))
    if s0 is not None:
        in_specs.append(state_spec)
        args.append(s0.reshape(n_layers, n_seq, groups, rp, n_state))
    aliases = {}
    for out_idx, buf in enumerate((y_buf, s_buf)):
        if buf is not None:
            aliases[len(args)] = out_idx
            in_specs.append(pl.BlockSpec(memory_space=pl.ANY))
            args.append(buf)
    y, s_new = pl.pallas_call(
        functools.partial(_ssd_kernel, q=q, n_sub=n_sub, r_heads=r_heads, p_dim=p_dim,
                          zero_init=s0 is None, n_alias=len(aliases)),
        grid=(n_seq // n_sub, groups, nc),
        in_specs=in_specs,
        out_specs=[pl.BlockSpec((rows_blk, rp), tok), state_spec],
        out_shape=[jax.ShapeDtypeStruct((m, d_inner), BF16),
                   jax.ShapeDtypeStruct((n_layers, n_seq, groups, rp, n_state), F32)],
        input_output_aliases=aliases,
        compiler_params=_params(("parallel", "parallel", "arbitrary")),
        name="ssd_scan_q%d" % q,
    )(*args)
    return y, s_new


def _tails(u, g, k):
    c = u.shape[1]
    last = np.arange(-k, 0)
    rows_p = ((np.arange(g.batch)[:, None] + 1) * g.seq + last[None, :]).reshape(-1)
    rows_s = (g.m_p + (np.arange(g.dec_batch)[:, None] + 1) * g.dec_seq + last[None, :]).reshape(-1)
    p = jnp.take(u, jnp.asarray(rows_p, jnp.int32), axis=0).reshape(g.batch, k, c)
    s = jnp.take(u, jnp.asarray(rows_s, jnp.int32), axis=0).reshape(g.dec_batch, k, c)
    return p, s


def kernel(x_prompt, x_sample, state_sconv, state_ssd_conv, state_ssd, c_prompt, c_sample, w_ada, b_ada, ln_g, ln_b, sc_w_in, sc_conv_w, sc_w_out, ssd_w_in, ssd_conv_w, ssd_conv_b, ssd_dt_bias, ssd_a_log, ssd_d, ssd_norm_g, ssd_w_out, ffn_w_gate, ffn_w_up, ffn_w_down):
    batch, seq, d = x_prompt.shape
    dec_batch, dec_seq, _ = x_sample.shape
    depth = w_ada.shape[0]
    g = _Slab(batch, seq, dec_batch, dec_seq, d)
    alpha = (2 * depth) ** 0.25

    n_ssd, _, heads, p_dim, n_state = state_ssd.shape
    d_inner = heads * p_dim
    conv_dim = ssd_conv_w.shape[-1]
    groups = (conv_dim - d_inner) // (2 * n_state)
    r_heads = heads // groups
    ffn_hidden = ffn_w_gate.shape[-1]

    n_c = batch + dec_batch
    n_c_pad = -(-n_c // BF16_ROWS) * BF16_ROWS
    c_all = jnp.concatenate([c_sample, c_prompt], axis=0)
    c_act = jnp.pad(_silu(c_all), ((0, n_c_pad - n_c), (0, 0))).astype(BF16)
    mod = _adaln(c_act, w_ada, b_ada)
    mod_p = [mod[l, dec_batch:n_c].reshape(batch, 1, N_MOD * d) for l in range(depth)]

    sc_w_in_b, sc_w_out_b = sc_w_in.astype(BF16), sc_w_out.astype(BF16)
    ssd_w_in_b, ssd_w_out_b = ssd_w_in.astype(BF16), ssd_w_out.astype(BF16)
    w_gate_b, w_up_b, w_down_b = ffn_w_gate.astype(BF16), ffn_w_up.astype(BF16), ffn_w_down.astype(BF16)

    x = jnp.concatenate([x_prompt.reshape(g.m_p, d), x_sample.reshape(g.m_s, d)], axis=0)
    h = _modulate(g, x, mod, mod_p, 0, 1, 0)

    sc_p, sc_s, cv_p, cv_s = [], [], [], []
    st_p = st_s = None
    for i in range(depth):
        j = i // 2
        if i % 2 == 0:
            gate_b, u = _matmul(h, [sc_w_in_b] * 3, j, [0, d, 2 * d], d, [BF16, F32],
                                epilogue="sconv", tm=1024, tn=512, tk=d, name="sconv_in")
            y = _causal_conv(g, u, state_sconv[j], sc_conv_w[j], None, gate_b, BF16, "sconv_conv")
            tp, ts = _tails(u, g, sc_conv_w.shape[1] - 1)
            sc_p.append(tp)
            sc_s.append(ts)
            (out,) = _matmul(y, [sc_w_out_b], j, [0], d, [BF16],
                             epilogue="plain", tm=1024, tn=1024, tk=4096, name="sconv_out")
        else:
            (z,) = _matmul(h, [ssd_w_in_b], j, [0], d_inner, [BF16],
                           epilogue="plain", tm=1024, tn=1024, tk=4096, name="ssd_in_z")
            (xbc,) = _matmul(h, [ssd_w_in_b], j, [d_inner], conv_dim, [F32],
                             epilogue="plain", tm=1024, tn=1024, tk=4096, name="ssd_in_xbc")
            (dt_raw,) = _matmul(h, [ssd_w_in_b], j, [d_inner + conv_dim], heads, [F32],
                                epilogue="plain", tm=1024, tn=heads, tk=4096, name="ssd_in_dt")
            tp, ts = _tails(xbc, g, ssd_conv_w.shape[1] - 1)
            cv_p.append(tp)
            cv_s.append(ts)
            xbc_act = _causal_conv(g, xbc, state_ssd_conv[j], ssd_conv_w[j], ssd_conv_b[j], None, F32, "ssd_conv")
            dt, adt = _dt_prep(dt_raw, ssd_dt_bias[j], ssd_a_log[j])
            d_skip_x = jnp.repeat(ssd_d[j], p_dim).reshape(1, d_inner)
            common = dict(m=g.m, n_layers=n_ssd, groups=groups, r_heads=r_heads, p_dim=p_dim, n_state=n_state)
            y, st_p = _ssd_scan(xbc_act, z, dt, adt, d_skip_x, ssd_norm_g[j], None, j, None, st_p,
                                row0=0, n_seq=batch, seq_len=seq, n_sub=1, **common)
            n_sub = _pick(dec_batch, max(1, SSD_CHUNK // dec_seq), 1) if dec_seq % SSD_CHUNK else 1
            y, st_s = _ssd_scan(xbc_act, z, dt, adt, d_skip_x, ssd_norm_g[j], state_ssd, j, y, st_s,
                                row0=g.m_p, n_seq=dec_batch, seq_len=dec_seq, n_sub=n_sub, **common)
            (out,) = _matmul(y, [ssd_w_out_b], j, [0], d, [BF16],
                             epilogue="plain", tm=1024, tn=1024, tk=4096, name="ssd_out")
        x, h = _ln_mod(g, x, out, mod, mod_p, i, 2, ln_g[i, 0], ln_b[i, 0], alpha, (i, 4, 3))

        (hid,) = _matmul(h, [w_gate_b, w_up_b], i, [0, 0], ffn_hidden, [BF16],
                         epilogue="swiglu", tm=1024, tn=512, tk=d, name="ffn_in", ragged_n=True)
        (out,) = _matmul(hid, [w_down_b], i, [0], d, [BF16],
                         epilogue="plain", tm=1024, tn=512, tk=5632, name="ffn_out")
        nxt = (i + 1, 1, 0) if i + 1 < depth else None
        x, h = _ln_mod(g, x, out, mod, mod_p, i, 5, ln_g[i, 1], ln_b[i, 1], alpha, nxt)

    y_prompt = x[:g.m_p].reshape(batch, seq, d)
    y_sample = x[g.m_p:].reshape(dec_batch, dec_seq, d)
    st_p = st_p.reshape(n_ssd, batch, heads, p_dim, n_state)
    st_s = st_s.reshape(n_ssd, dec_batch, heads, p_dim, n_state)
    return (y_prompt, y_sample, jnp.stack(sc_p), jnp.stack(cv_p), st_p,
            jnp.stack(sc_s), jnp.stack(cv_s), st_s)
```

```python
import functools
import math

import numpy as np
import jax
import jax.numpy as jnp
from jax import lax
from jax.experimental import pallas as pl
from jax.experimental.pallas import tpu as pltpu

F32 = jnp.float32
BF16 = jnp.bfloat16

LN_EPS = 1e-5
RMS_EPS = 1e-5
SSD_CHUNK = 128
N_MOD = 6

V7X_VMEM_LIMIT_BYTES = 56 * 1024 * 1024
LANE = 128
SUBLANE = 8
BF16_ROWS = 2 * SUBLANE
FIX_ROWS = BF16_ROWS
SSD_PROMPT_GROUPS_PER_STEP = 4
CONV_TILE_ROWS = 256
CONV_TILE_COLS = 5120


def _pick(dim, pref, align):
    best = None
    t = align
    while t <= min(dim, pref):
        if dim % t == 0:
            best = t
        t += align
    return best if best is not None else dim


def _params(sem):
    return pltpu.CompilerParams(dimension_semantics=sem, vmem_limit_bytes=V7X_VMEM_LIMIT_BYTES)


def _silu(x):
    return x * jax.nn.sigmoid(x)


def _split3(v):
    hi = v.astype(BF16)
    r1 = v - hi.astype(F32)
    mid = r1.astype(BF16)
    lo = (r1 - mid.astype(F32)).astype(BF16)
    return hi, mid, lo


def _dot_exact_lhs(sel, v):
    return sum(jnp.dot(sel, p, preferred_element_type=F32) for p in _split3(v))


def _dot_exact_rhs(v, sel):
    return sum(jnp.dot(p, sel, preferred_element_type=F32) for p in _split3(v))


def _mm_kernel(*refs, n_w, n_out, nk, epilogue):
    a_ref = refs[0]
    w_refs = refs[1:1 + n_w]
    o_refs = refs[1 + n_w:1 + n_w + n_out]
    acc_refs = refs[1 + n_w + n_out:]

    def finish(accs):
        if epilogue == "plain":
            o_refs[0][...] = accs[0].astype(o_refs[0].dtype)
        elif epilogue == "swiglu":
            o_refs[0][...] = (_silu(accs[0]) * accs[1]).astype(o_refs[0].dtype)
        else:
            o_refs[0][...] = accs[0].astype(o_refs[0].dtype)
            o_refs[1][...] = (accs[1] * accs[2]).astype(o_refs[1].dtype)

    a = a_ref[...]
    parts = [jnp.dot(a, w[...], preferred_element_type=F32) for w in w_refs]
    if nk == 1:
        finish(parts)
        return
    k = pl.program_id(2)

    @pl.when(k == 0)
    def _():
        for acc, p in zip(acc_refs, parts):
            acc[...] = p

    @pl.when(k > 0)
    def _():
        for acc, p in zip(acc_refs, parts):
            acc[...] += p

    @pl.when(k == nk - 1)
    def _():
        finish([acc[...] for acc in acc_refs])


def _matmul(a, ws, layer, col_offsets, n, out_dtypes, *, epilogue, tm, tn, tk, name, ragged_n=False):
    m, kdim = a.shape
    tm = _pick(m, tm, BF16_ROWS)
    tn = min(tn, n) if ragged_n else _pick(n, tn, LANE)
    tk = _pick(kdim, tk, LANE)
    nk = kdim // tk
    for off in col_offsets:
        assert off % tn == 0
    n_w = len(ws)
    n_out = len(out_dtypes)
    in_specs = [pl.BlockSpec((tm, tk), lambda i, j, k: (i, k))]
    for off in col_offsets:
        in_specs.append(pl.BlockSpec((None, tk, tn),
                                     functools.partial(lambda i, j, k, o: (layer, k, j + o), o=off // tn)))
    out_specs = [pl.BlockSpec((tm, tn), lambda i, j, k: (i, j)) for _ in out_dtypes]
    scratch = [pltpu.VMEM((tm, tn), F32) for _ in ws] if nk > 1 else []
    outs = pl.pallas_call(
        functools.partial(_mm_kernel, n_w=n_w, n_out=n_out, nk=nk, epilogue=epilogue),
        grid=(m // tm, pl.cdiv(n, tn), nk),
        in_specs=in_specs,
        out_specs=out_specs,
        out_shape=[jax.ShapeDtypeStruct((m, n), dt) for dt in out_dtypes],
        scratch_shapes=scratch,
        compiler_params=_params(("parallel", "parallel", "arbitrary")),
        name=name,
    )(a, *ws)
    return outs


def _ada_kernel(c_ref, w_ref, b_ref, o_ref):
    w = w_ref[...].astype(BF16)
    o_ref[...] = jnp.dot(c_ref[...], w, preferred_element_type=F32) + b_ref[...]


def _adaln(c_act, w_ada, b_ada):
    depth, d, n = w_ada.shape
    rows = c_act.shape[0]
    tn = _pick(n, 512, LANE)
    return pl.pallas_call(
        _ada_kernel,
        grid=(depth, n // tn),
        in_specs=[
            pl.BlockSpec((rows, d), lambda l, j: (0, 0)),
            pl.BlockSpec((None, d, tn), lambda l, j: (l, 0, j)),
            pl.BlockSpec((None, 1, tn), lambda l, j: (l, 0, j)),
        ],
        out_specs=pl.BlockSpec((None, rows, tn), lambda l, j: (l, 0, j)),
        out_shape=jax.ShapeDtypeStruct((depth, rows, n), F32),
        compiler_params=_params(("parallel", "parallel")),
        name="adaln",
    )(c_act, w_ada, b_ada.reshape(depth, 1, n))


class _Slab:
    def __init__(self, batch, seq, dec_batch, dec_seq, d):
        self.batch, self.seq, self.dec_batch, self.dec_seq, self.d = batch, seq, dec_batch, dec_seq, d
        self.m_p = batch * seq
        self.m_s = dec_batch * dec_seq
        self.m = self.m_p + self.m_s
        align = math.lcm(dec_seq * SUBLANE, BF16_ROWS)
        self.tm = _pick(math.gcd(seq, self.m_s), 256, align)
        self.n_p = self.m_p // self.tm
        self.n_t = self.m // self.tm
        self.tiles_per_seq = seq // self.tm
        self.tm_conv = _pick(math.gcd(seq, self.m_s), CONV_TILE_ROWS, align)


def _mod_specs(g, layer, chunk):
    tm, d = g.tm, g.d
    p_spec = pl.BlockSpec((None, 1, d),
                          lambda i: (jnp.minimum(i // g.tiles_per_seq, g.batch - 1), 0, chunk))
    s_spec = pl.BlockSpec((None, tm // g.dec_seq, d), lambda i: (layer, jnp.maximum(i - g.n_p, 0), chunk))
    return p_spec, s_spec


def _seq_to_tokens(v, tm, dec_seq):
    n_seq = tm // dec_seq
    r = lax.broadcasted_iota(jnp.int32, (tm, n_seq), 0) // dec_seq
    c = lax.broadcasted_iota(jnp.int32, (tm, n_seq), 1)
    sel = jnp.where(r == c, 1.0, 0.0).astype(BF16)
    return _dot_exact_lhs(sel, v)


def _modulate_kernel(xp_ref, xs_ref, scp_ref, shp_ref, scs_ref, shs_ref, x_ref, h_ref, *, n_p, dec_seq):
    i = pl.program_id(0)
    tm = x_ref.shape[0]

    @pl.when(i < n_p)
    def _():
        x = xp_ref[...]
        x_ref[...] = x
        h_ref[...] = (x * (1.0 + scp_ref[...]) + shp_ref[...]).astype(h_ref.dtype)

    @pl.when(i >= n_p)
    def _():
        x = xs_ref[...]
        x_ref[...] = x
        sc = _seq_to_tokens(scs_ref[...], tm, dec_seq)
        sh = _seq_to_tokens(shs_ref[...], tm, dec_seq)
        h_ref[...] = (x * (1.0 + sc) + sh).astype(h_ref.dtype)


def _ln_kernel(*refs, n_p, dec_seq, alpha, with_next):
    if with_next:
        (x_ref, o_ref, gp_ref, gs_ref, lg_ref, lb_ref, scp_ref, shp_ref, scs_ref, shs_ref,
         xo_ref, h_ref) = refs
    else:
        x_ref, o_ref, gp_ref, gs_ref, lg_ref, lb_ref, xo_p_ref, xo_s_ref = refs
    i = pl.program_id(0)
    tm = x_ref.shape[0]

    def body(g, sc, sh, out_ref):
        v = alpha * x_ref[...] + g * o_ref[...]
        mu = jnp.mean(v, axis=-1, keepdims=True)
        vc = v - mu
        var = jnp.mean(vc * vc, axis=-1, keepdims=True)
        xn = vc * lax.rsqrt(var + LN_EPS) * lg_ref[...] + lb_ref[...]
        out_ref[...] = xn
        if with_next:
            h_ref[...] = (xn * (1.0 + sc) + sh).astype(h_ref.dtype)

    @pl.when(i < n_p)
    def _():
        body(gp_ref[...], scp_ref[...] if with_next else None, shp_ref[...] if with_next else None,
             xo_ref if with_next else xo_p_ref)

    @pl.when(i >= n_p)
    def _():
        body(_seq_to_tokens(gs_ref[...], tm, dec_seq),
             _seq_to_tokens(scs_ref[...], tm, dec_seq) if with_next else None,
             _seq_to_tokens(shs_ref[...], tm, dec_seq) if with_next else None,
             xo_ref if with_next else xo_s_ref)


def _split_specs(g):
    tm, d = g.tm, g.d
    p_spec = pl.BlockSpec((tm, d), lambda i: (jnp.minimum(i, g.n_p - 1), 0))
    s_spec = pl.BlockSpec((tm, d), lambda i: (jnp.maximum(i - g.n_p, 0), 0))
    return p_spec, s_spec


def _modulate(g, x_p, x_s, mod, mod_p, layer, chunk_sc, chunk_sh):
    tm, d = g.tm, g.d
    scp, scs = _mod_specs(g, layer, chunk_sc)
    shp, shs = _mod_specs(g, layer, chunk_sh)
    xp_spec, xs_spec = _split_specs(g)
    row = pl.BlockSpec((tm, d), lambda i: (i, 0))
    return pl.pallas_call(
        functools.partial(_modulate_kernel, n_p=g.n_p, dec_seq=g.dec_seq),
        grid=(g.n_t,),
        in_specs=[xp_spec, xs_spec, scp, shp, scs, shs],
        out_specs=[row, row],
        out_shape=[jax.ShapeDtypeStruct((g.m, d), F32), jax.ShapeDtypeStruct((g.m, d), BF16)],
        compiler_params=_params(("arbitrary",)),
        name="modulate",
    )(x_p, x_s, mod_p[layer], mod_p[layer], mod, mod)


def _ln_mod(g, x, o, mod, mod_p, layer, chunk_g, ln_g, ln_b, alpha, nxt):
    tm, d = g.tm, g.d
    row = pl.BlockSpec((tm, d), lambda i: (i, 0))
    vec = pl.BlockSpec((1, d), lambda i: (0, 0))
    gp, gs = _mod_specs(g, layer, chunk_g)
    in_specs = [row, row, gp, gs, vec, vec]
    args = [x, o, mod_p[layer], mod, ln_g.reshape(1, d), ln_b.reshape(1, d)]
    if nxt is not None:
        n_layer, c_sc, c_sh = nxt
        scp, scs = _mod_specs(g, n_layer, c_sc)
        shp, shs = _mod_specs(g, n_layer, c_sh)
        in_specs += [scp, shp, scs, shs]
        args += [mod_p[n_layer], mod_p[n_layer], mod, mod]
        out_specs = [row, row]
        out_shape = [jax.ShapeDtypeStruct((g.m, d), F32), jax.ShapeDtypeStruct((g.m, d), BF16)]
    else:
        out_specs = list(_split_specs(g))
        out_shape = [jax.ShapeDtypeStruct((g.m_p, d), F32), jax.ShapeDtypeStruct((g.m_s, d), F32)]
    return pl.pallas_call(
        functools.partial(_ln_kernel, n_p=g.n_p, dec_seq=g.dec_seq, alpha=alpha, with_next=nxt is not None),
        grid=(g.n_t,),
        in_specs=in_specs,
        out_specs=out_specs,
        out_shape=out_shape,
        compiler_params=_params(("arbitrary",)),
        name="ln_mod",
    )(*args)


def _conv_kernel(*refs, n_p, tiles_per_seq, width, dec_seq, with_bias_silu, with_gate):
    u_ref, halo_ref, e_ref, w_ref = refs[:4]
    pos = 4
    b_ref = g_ref = None
    if with_bias_silu:
        b_ref = refs[pos]
        pos += 1
    if with_gate:
        g_ref = refs[pos]
        pos += 1
    o_ref = refs[pos]
    i = pl.program_id(0)
    w = w_ref[...]

    def post(y, rows):
        if with_bias_silu:
            y = _silu(y + b_ref[...])
        if with_gate:
            y = y * g_ref[rows, :].astype(F32)
        return y.astype(o_ref.dtype)

    u = u_ref[...]

    @pl.when(i < n_p)
    def _():
        y = u * w[width - 1:width, :]
        for s in range(1, width):
            y = y + pltpu.roll(u, s, 0) * w[width - 1 - s:width - s, :]
        o_ref[...] = post(y, slice(None))
        keep = jnp.where(i % tiles_per_seq == 0, 0.0, 1.0)
        halo = halo_ref[...] * keep
        halo = jnp.concatenate([halo, halo], axis=0)
        top = u[0:FIX_ROWS, :]
        r = lax.broadcasted_iota(jnp.int32, top.shape, 0)
        yt = top * w[width - 1:width, :]
        for s in range(1, width):
            sh = jnp.where(r < s, pltpu.roll(halo, s, 0), pltpu.roll(top, s, 0))
            yt = yt + sh * w[width - 1 - s:width - s, :]
        o_ref[0:FIX_ROWS, :] = post(yt, slice(0, FIX_ROWS))

    @pl.when(i >= n_p)
    def _():
        tm = u.shape[0]
        t = lax.broadcasted_iota(jnp.int32, u.shape, 0) % dec_seq
        e = e_ref[...]
        y = u * w[width - 1:width, :]
        for s in range(1, width):
            back = width - 1 - s
            src = pltpu.roll(e, tm - back, 0) if back else e
            sh = jnp.where(t >= s, pltpu.roll(u, s, 0), src)
            y = y + sh * w[width - 1 - s:width - s, :]
        o_ref[...] = post(y, slice(None))


def _state_rows(buf, dec_seq):
    n_seq, km1, c = buf.shape
    assert dec_seq >= km1
    return jnp.pad(buf, ((0, 0), (0, dec_seq - km1), (0, 0))).reshape(n_seq * dec_seq, c)


def _causal_conv(g, u, buf_s, w, bias, gate, out_dtype, name):
    m, c = u.shape
    width = w.shape[0]
    tm = g.tm_conv
    tc = _pick(c, CONV_TILE_COLS, LANE)
    n_p = g.m_p // tm
    tile = pl.BlockSpec((tm, tc), lambda i, j: (i, j))
    halo = pl.BlockSpec((SUBLANE, tc), lambda i, j: (jnp.maximum(i * (tm // SUBLANE) - 1, 0), j))
    e_spec = pl.BlockSpec((tm, tc), lambda i, j: (jnp.maximum(i - n_p, 0), j))
    in_specs = [tile, halo, e_spec, pl.BlockSpec((width, tc), lambda i, j: (0, j))]
    args = [u, u, _state_rows(buf_s, g.dec_seq), w]
    if bias is not None:
        in_specs.append(pl.BlockSpec((1, tc), lambda i, j: (0, j)))
        args.append(bias.reshape(1, c))
    if gate is not None:
        in_specs.append(tile)
        args.append(gate)
    return pl.pallas_call(
        functools.partial(_conv_kernel, n_p=n_p, tiles_per_seq=g.seq // tm, width=width,
                          dec_seq=g.dec_seq, with_bias_silu=bias is not None, with_gate=gate is not None),
        grid=(m // tm, c // tc),
        in_specs=in_specs,
        out_specs=tile,
        out_shape=jax.ShapeDtypeStruct((m, c), out_dtype),
        compiler_params=_params(("parallel", "parallel")),
        name=name,
    )(*args)


def _dt_kernel(r_ref, b_ref, al_ref, dt_ref, adt_ref):
    dt = jax.nn.softplus(r_ref[...] + b_ref[...])
    dt_ref[...] = dt
    adt_ref[...] = dt * (-jnp.exp(al_ref[...]))


def _dt_prep(dt_raw, dt_bias, a_log):
    m, h = dt_raw.shape
    tm = _pick(m, 1024, SUBLANE)
    row = pl.BlockSpec((tm, h), lambda i: (i, 0))
    vec = pl.BlockSpec((1, h), lambda i: (0, 0))
    return pl.pallas_call(
        _dt_kernel,
        grid=(m // tm,),
        in_specs=[row, vec, vec],
        out_specs=[row, row],
        out_shape=[jax.ShapeDtypeStruct((m, h), F32)] * 2,
        compiler_params=_params(("parallel",)),
        name="ssd_dt",
    )(dt_raw, dt_bias.reshape(1, h), a_log.reshape(1, h))


def _ssd_kernel(*refs, q, n_sub, g_blk, r_heads, p_dim, zero_init, n_alias):
    (xs_ref, b_ref, c_ref, z_ref, dtc_ref, adtc_ref, adtr_ref, dsk_ref, ng_ref, ex_ref) = refs[:10]
    pos = 10
    s0_ref = None
    if not zero_init:
        s0_ref = refs[pos]
        pos += 1
    pos += n_alias
    y_ref, s_ref = refs[pos], refs[pos + 1]
    ci = pl.program_id(2)

    @pl.when(ci == 0)
    def _():
        if zero_init:
            s_ref[...] = jnp.zeros_like(s_ref)
        else:
            s_ref[...] = s0_ref[...]

    rows = n_sub * q
    row = lax.broadcasted_iota(jnp.int32, (rows, rows), 0)
    col = lax.broadcasted_iota(jnp.int32, (rows, rows), 1)
    if n_sub == 1:
        causal = row >= col
        anti = row <= col
    else:
        same = (row // q) == (col // q)
        causal = jnp.logical_and(same, row >= col)
        anti = jnp.logical_and(same, row <= col)
    tril = jnp.where(causal, 1.0, 0.0).astype(BF16)
    triu = jnp.where(anti, 1.0, 0.0).astype(BF16)
    for gi in range(g_blk):
        _ssd_group(gi, causal, tril, triu, xs_ref, b_ref, c_ref, z_ref, dtc_ref, adtc_ref, adtr_ref, dsk_ref,
                   ng_ref, ex_ref, y_ref, s_ref, q=q, n_sub=n_sub, r_heads=r_heads, p_dim=p_dim)


def _ssd_group(gi, causal, tril, triu, xs_ref, b_ref, c_ref, z_ref, dtc_ref, adtc_ref, adtr_ref, dsk_ref,
               ng_ref, ex_ref, y_ref, s_ref, *, q, n_sub, r_heads, p_dim):
    rows = n_sub * q
    n_state = s_ref.shape[-1]
    rp = r_heads * p_dim
    ch = slice(gi * rp, (gi + 1) * rp)
    st = slice(gi * n_state, (gi + 1) * n_state)
    xs = xs_ref[:, ch]
    b_f = b_ref[:, st]
    c_f = c_ref[:, st]
    bm = b_f.astype(BF16)
    cm = c_f.astype(BF16)
    dt_c = dtc_ref[gi]
    adt_c = adtc_ref[gi]
    adt_r = adtr_ref[gi]
    cs_c = _dot_exact_lhs(tril, adt_c)
    cs_r = _dot_exact_rhs(adt_r, triu)
    lasts = [cs_c[k * q + q - 1:k * q + q, :] for k in range(n_sub)]
    if n_sub == 1:
        cs_last = lasts[0]
    else:
        cs_last = jnp.concatenate([jnp.broadcast_to(v, (q, r_heads)) for v in lasts], axis=0)

    stacked = jnp.concatenate([dt_c, jnp.exp(cs_c), jnp.exp(cs_last - cs_c)], axis=0)
    wide = _dot_exact_rhs(stacked, ex_ref[...])
    dt_x, ecs_x, dec_x = wide[0:rows], wide[rows:2 * rows], wide[2 * rows:3 * rows]

    xdt = xs * dt_x
    xb = xdt.astype(BF16)
    xdec = xdt * dec_x

    y_state = []
    for k in range(n_sub):
        seq = slice(k * q, (k + 1) * q)
        s_old = s_ref[k, gi]
        y_state.append(lax.dot_general(c_f[seq].astype(BF16), s_old.astype(BF16), (((1,), (1,)), ((), ())),
                                       preferred_element_type=F32))
        upd = lax.dot_general(xdec[seq].astype(BF16), b_f[seq].astype(BF16), (((0,), (0,)), ((), ())),
                              preferred_element_type=F32)
        e_heads = jnp.broadcast_to(jnp.exp(cs_r[:, k * q + q - 1:k * q + q]), (r_heads, n_state))
        for h in range(r_heads):
            hp = slice(h * p_dim, (h + 1) * p_dim)
            s_ref[k, gi, hp, :] = s_old[hp] * e_heads[h:h + 1, :] + upd[hp]
    y = (y_state[0] if n_sub == 1 else jnp.concatenate(y_state, axis=0)) * ecs_x

    cb = lax.dot_general(cm, bm, (((1,), (1,)), ((), ())), preferred_element_type=F32)
    pieces = []
    heads_per_blk = max(1, LANE // p_dim)
    assert r_heads % heads_per_blk == 0
    for blk in range(r_heads // heads_per_blk):
        width = heads_per_blk * p_dim
        xblk = xb[:, blk * width:(blk + 1) * width]
        lane = lax.broadcasted_iota(jnp.int32, (rows, width), 1)
        acc = None
        for k in range(heads_per_blk):
            h = blk * heads_per_blk + k
            diff = cs_c[:, h:h + 1] - cs_r[h:h + 1, :]
            lmat = jnp.exp(jnp.where(causal, diff, -jnp.inf))
            yk = jnp.dot((cb * lmat).astype(BF16), xblk, preferred_element_type=F32)
            acc = yk if acc is None else jnp.where(lane >= k * p_dim, yk, acc)
        pieces.append(acc)
    y = y + (pieces[0] if len(pieces) == 1 else jnp.concatenate(pieces, axis=1))

    y = y + dsk_ref[:, ch] * xs
    yz = y * _silu(z_ref[:, ch].astype(F32))
    ms = jnp.mean(yz * yz, axis=-1, keepdims=True)
    y_ref[:, ch] = (yz * lax.rsqrt(ms + RMS_EPS) * ng_ref[:, ch]).astype(y_ref.dtype)


def _ssd_scan(xbc, z, dt, adt, d_skip_x, norm_g, s0, layer, y_buf, s_buf, *, m, n_layers, row0, n_seq,
              seq_len, n_sub, g_blk, groups, r_heads, p_dim, n_state):
    q = SSD_CHUNK if seq_len % SSD_CHUNK == 0 else seq_len
    nc = seq_len // q
    rp = r_heads * p_dim
    d_inner = groups * rp
    rows_blk = n_sub * q
    assert nc == 1 or n_sub == 1
    assert row0 % rows_blk == 0 and n_seq % n_sub == 0 and rows_blk % BF16_ROWS == 0
    assert rp % LANE == 0 and n_state % LANE == 0 and d_inner % n_state == 0
    blk0 = row0 // rows_blk
    rows = n_seq * seq_len

    def per_group(v):
        v = v[row0:row0 + rows].reshape(rows // rows_blk, rows_blk, groups, r_heads)
        return v.transpose(2, 0, 1, 3), v.transpose(2, 0, 3, 1)

    dt_c, _ = per_group(dt)
    adt_c, adt_r = per_group(adt)
    head_of = np.arange(rp) // p_dim
    expand = jnp.asarray(head_of[None, :] == np.arange(r_heads)[:, None], dtype=BF16)

    assert groups % g_blk == 0 and (d_inner // n_state) % g_blk == 0
    b_col0 = d_inner // (g_blk * n_state)
    c_col0 = b_col0 + groups // g_blk
    tok = lambda s, g, c: (blk0 + s * nc + c, g)
    per_chunk = lambda s, g, c: (g, s * nc + c, 0, 0)
    const = lambda s, g, c: (0, 0)
    in_specs = [
        pl.BlockSpec((rows_blk, g_blk * rp), tok),
        pl.BlockSpec((rows_blk, g_blk * n_state), lambda s, g, c: (blk0 + s * nc + c, b_col0 + g)),
        pl.BlockSpec((rows_blk, g_blk * n_state), lambda s, g, c: (blk0 + s * nc + c, c_col0 + g)),
        pl.BlockSpec((rows_blk, g_blk * rp), tok),
        pl.BlockSpec((g_blk, None, rows_blk, r_heads), per_chunk),
        pl.BlockSpec((g_blk, None, rows_blk, r_heads), per_chunk),
        pl.BlockSpec((g_blk, None, r_heads, rows_blk), per_chunk),
        pl.BlockSpec((1, g_blk * rp), lambda s, g, c: (0, g)),
        pl.BlockSpec((1, g_blk * rp), lambda s, g, c: (0, g)),
        pl.BlockSpec((r_heads, rp), const),
    ]
    args = [xbc, xbc, xbc, z, dt_c, adt_c, adt_r, d_skip_x, norm_g.reshape(1, d_inner), expand]
    state_spec = pl.BlockSpec((None, n_sub, g_blk, rp, n_state), lambda s, g, c: (layer, s, g, 0, 0))
    if s0 is not None:
        in_specs.append(state_spec)
        args.append(s0.reshape(n_layers, n_seq, groups, rp, n_state))
    aliases = {}
    for out_idx, buf in enumerate((y_buf, s_buf)):
        if buf is not None:
            aliases[len(args)] = out_idx
            in_specs.append(pl.BlockSpec(memory_space=pl.ANY))
            args.append(buf)
    y, s_new = pl.pallas_call(
        functools.partial(_ssd_kernel, q=q, n_sub=n_sub, g_blk=g_blk, r_heads=r_heads, p_dim=p_dim,
                          zero_init=s0 is None, n_alias=len(aliases)),
        grid=(n_seq // n_sub, groups // g_blk, nc),
        in_specs=in_specs,
        out_specs=[pl.BlockSpec((rows_blk, g_blk * rp), tok), state_spec],
        out_shape=[jax.ShapeDtypeStruct((m, d_inner), BF16),
                   jax.ShapeDtypeStruct((n_layers, n_seq, groups, rp, n_state), F32)],
        input_output_aliases=aliases,
        compiler_params=_params(("parallel", "parallel", "arbitrary")),
        name="ssd_scan_q%d" % q,
    )(*args)
    return y, s_new


def _tails(u, g, k):
    c = u.shape[1]
    last = np.arange(-k, 0)
    rows_p = ((np.arange(g.batch)[:, None] + 1) * g.seq + last[None, :]).reshape(-1)
    rows_s = (g.m_p + (np.arange(g.dec_batch)[:, None] + 1) * g.dec_seq + last[None, :]).reshape(-1)
    p = jnp.take(u, jnp.asarray(rows_p, jnp.int32), axis=0).reshape(g.batch, k, c)
    s = jnp.take(u, jnp.asarray(rows_s, jnp.int32), axis=0).reshape(g.dec_batch, k, c)
    return p, s


def kernel(x_prompt, x_sample, state_sconv, state_ssd_conv, state_ssd, c_prompt, c_sample, w_ada, b_ada, ln_g, ln_b, sc_w_in, sc_conv_w, sc_w_out, ssd_w_in, ssd_conv_w, ssd_conv_b, ssd_dt_bias, ssd_a_log, ssd_d, ssd_norm_g, ssd_w_out, ffn_w_gate, ffn_w_up, ffn_w_down):
    batch, seq, d = x_prompt.shape
    dec_batch, dec_seq, _ = x_sample.shape
    depth = w_ada.shape[0]
    g = _Slab(batch, seq, dec_batch, dec_seq, d)
    alpha = (2 * depth) ** 0.25

    n_ssd, _, heads, p_dim, n_state = state_ssd.shape
    d_inner = heads * p_dim
    conv_dim = ssd_conv_w.shape[-1]
    groups = (conv_dim - d_inner) // (2 * n_state)
    r_heads = heads // groups
    ffn_hidden = ffn_w_gate.shape[-1]

    n_c = batch + dec_batch
    n_c_pad = -(-n_c // BF16_ROWS) * BF16_ROWS
    c_all = jnp.concatenate([c_sample, c_prompt], axis=0)
    c_act = jnp.pad(_silu(c_all), ((0, n_c_pad - n_c), (0, 0))).astype(BF16)
    mod = _adaln(c_act, w_ada, b_ada)
    mod_p = [mod[l, dec_batch:n_c].reshape(batch, 1, N_MOD * d) for l in range(depth)]

    sc_w_in_b, sc_w_out_b = sc_w_in.astype(BF16), sc_w_out.astype(BF16)
    ssd_w_in_b, ssd_w_out_b = ssd_w_in.astype(BF16), ssd_w_out.astype(BF16)
    w_gate_b, w_up_b, w_down_b = ffn_w_gate.astype(BF16), ffn_w_up.astype(BF16), ffn_w_down.astype(BF16)

    x, h = _modulate(g, x_prompt.reshape(g.m_p, d), x_sample.reshape(g.m_s, d), mod, mod_p, 0, 1, 0)

    sc_p, sc_s, cv_p, cv_s = [], [], [], []
    st_p = st_s = None
    for i in range(depth):
        j = i // 2
        if i % 2 == 0:
            gate_b, u = _matmul(h, [sc_w_in_b] * 3, j, [0, d, 2 * d], d, [BF16, F32],
                                epilogue="sconv", tm=1024, tn=512, tk=d, name="sconv_in")
            y = _causal_conv(g, u, state_sconv[j], sc_conv_w[j], None, gate_b, BF16, "sconv_conv")
            tp, ts = _tails(u, g, sc_conv_w.shape[1] - 1)
            sc_p.append(tp)
            sc_s.append(ts)
            (out,) = _matmul(y, [sc_w_out_b], j, [0], d, [BF16],
                             epilogue="plain", tm=1024, tn=1024, tk=4096, name="sconv_out")
        else:
            (z,) = _matmul(h, [ssd_w_in_b], j, [0], d_inner, [BF16],
                           epilogue="plain", tm=1024, tn=1024, tk=4096, name="ssd_in_z")
            (xbc,) = _matmul(h, [ssd_w_in_b], j, [d_inner], conv_dim, [F32],
                             epilogue="plain", tm=1024, tn=1024, tk=4096, name="ssd_in_xbc")
            (dt_raw,) = _matmul(h, [ssd_w_in_b], j, [d_inner + conv_dim], heads, [F32],
                                epilogue="plain", tm=1024, tn=heads, tk=4096, name="ssd_in_dt")
            tp, ts = _tails(xbc, g, ssd_conv_w.shape[1] - 1)
            cv_p.append(tp)
            cv_s.append(ts)
            xbc_act = _causal_conv(g, xbc, state_ssd_conv[j], ssd_conv_w[j], ssd_conv_b[j], None, F32, "ssd_conv")
            dt, adt = _dt_prep(dt_raw, ssd_dt_bias[j], ssd_a_log[j])
            d_skip_x = jnp.repeat(ssd_d[j], p_dim).reshape(1, d_inner)
            common = dict(m=g.m, n_layers=n_ssd, groups=groups, r_heads=r_heads, p_dim=p_dim, n_state=n_state)
            y, st_p = _ssd_scan(xbc_act, z, dt, adt, d_skip_x, ssd_norm_g[j], None, j, None, st_p,
                                row0=0, n_seq=batch, seq_len=seq, n_sub=1,
                                g_blk=_pick(groups, SSD_PROMPT_GROUPS_PER_STEP, 1), **common)
            n_sub = _pick(dec_batch, max(1, SSD_CHUNK // dec_seq), 1) if dec_seq % SSD_CHUNK else 1
            y, st_s = _ssd_scan(xbc_act, z, dt, adt, d_skip_x, ssd_norm_g[j], state_ssd, j, y, st_s,
                                row0=g.m_p, n_seq=dec_batch, seq_len=dec_seq, n_sub=n_sub, g_blk=1, **common)
            (out,) = _matmul(y, [ssd_w_out_b], j, [0], d, [BF16],
                             epilogue="plain", tm=1024, tn=1024, tk=4096, name="ssd_out")
        x, h = _ln_mod(g, x, out, mod, mod_p, i, 2, ln_g[i, 0], ln_b[i, 0], alpha, (i, 4, 3))

        (hid,) = _matmul(h, [w_gate_b, w_up_b], i, [0, 0], ffn_hidden, [BF16],
                         epilogue="swiglu", tm=1024, tn=512, tk=d, name="ffn_in", ragged_n=True)
        (out,) = _matmul(hid, [w_down_b], i, [0], d, [BF16],
                         epilogue="plain", tm=1024, tn=512, tk=5632, name="ffn_out")
        nxt = (i + 1, 1, 0) if i + 1 < depth else None
        x, h = _ln_mod(g, x, out, mod, mod_p, i, 5, ln_g[i, 1], ln_b[i, 1], alpha, nxt)

    rows_prompt, rows_sample = x, h
    y_prompt = rows_prompt.reshape(batch, seq, d)
    y_sample = rows_sample.reshape(dec_batch, dec_seq, d)
    st_p = st_p.reshape(n_ssd, batch, heads, p_dim, n_state)
    st_s = st_s.reshape(n_ssd, dec_batch, heads, p_dim, n_state)
    return (y_prompt, y_sample, jnp.stack(sc_p), jnp.stack(cv_p), st_p,
            jnp.stack(sc_s), jnp.stack(cv_s), st_s)
```

```python
import functools
import math

import numpy as np
import jax
import jax.numpy as jnp
from jax import lax
from jax.experimental import pallas as pl
from jax.experimental.pallas import tpu as pltpu

F32 = jnp.float32
BF16 = jnp.bfloat16

LN_EPS = 1e-5
RMS_EPS = 1e-5
SSD_CHUNK = 128
N_MOD = 6

V7X_VMEM_LIMIT_BYTES = 56 * 1024 * 1024
LANE = 128
SUBLANE = 8
BF16_ROWS = 2 * SUBLANE
FIX_ROWS = BF16_ROWS
SSD_PROMPT_GROUPS_PER_STEP = 4
CONV_TILE_ROWS = 256
CONV_TILE_COLS = 5120


def _pick(dim, pref, align):
    best = None
    t = align
    while t <= min(dim, pref):
        if dim % t == 0:
            best = t
        t += align
    return best if best is not None else dim


def _params(sem):
    return pltpu.CompilerParams(dimension_semantics=sem, vmem_limit_bytes=V7X_VMEM_LIMIT_BYTES)


def _silu(x):
    return x * jax.nn.sigmoid(x)


def _split3(v):
    hi = v.astype(BF16)
    r1 = v - hi.astype(F32)
    mid = r1.astype(BF16)
    lo = (r1 - mid.astype(F32)).astype(BF16)
    return hi, mid, lo


def _dot_exact_lhs(sel, v):
    return sum(jnp.dot(sel, p, preferred_element_type=F32) for p in _split3(v))


def _dot_exact_rhs(v, sel):
    return sum(jnp.dot(p, sel, preferred_element_type=F32) for p in _split3(v))


def _mm_kernel(*refs, n_w, n_side, n_out, nk, epilogue):
    a_ref = refs[0]
    w_refs = refs[1:1 + n_w]
    pos = 1 + n_w
    side_in = refs[pos:pos + n_side]
    pos += n_side
    o_refs = refs[pos:pos + n_out]
    pos += n_out
    side_out = refs[pos:pos + n_side]
    acc_refs = refs[pos + n_side:]

    for src, dst in zip(side_in, side_out):
        dst[...] = src[...].astype(dst.dtype)

    def finish(accs):
        if epilogue == "plain":
            o_refs[0][...] = accs[0].astype(o_refs[0].dtype)
        elif epilogue == "swiglu":
            o_refs[0][...] = (_silu(accs[0]) * accs[1]).astype(o_refs[0].dtype)
        else:
            o_refs[0][...] = accs[0].astype(o_refs[0].dtype)
            o_refs[1][...] = (accs[1] * accs[2]).astype(o_refs[1].dtype)

    a = a_ref[...]
    parts = [jnp.dot(a, w[...], preferred_element_type=F32) for w in w_refs]
    if nk == 1:
        finish(parts)
        return
    k = pl.program_id(2)

    @pl.when(k == 0)
    def _():
        for acc, p in zip(acc_refs, parts):
            acc[...] = p

    @pl.when(k > 0)
    def _():
        for acc, p in zip(acc_refs, parts):
            acc[...] += p

    @pl.when(k == nk - 1)
    def _():
        finish([acc[...] for acc in acc_refs])


def _matmul(a, ws, col_offsets, n, out_dtypes, *, epilogue, tm, tn, tk, name, ragged_n=False, side=()):
    m, kdim = a.shape
    tm = _pick(m, tm, BF16_ROWS)
    tn = min(tn, n) if ragged_n else _pick(n, tn, LANE)
    tk = _pick(kdim, tk, LANE)
    nk = kdim // tk
    for off in col_offsets:
        assert off % tn == 0
    n_w = len(ws)
    n_out = len(out_dtypes)
    nj = pl.cdiv(n, tn)
    steps = (m // tm) * nj * nk
    in_specs = [pl.BlockSpec((tm, tk), lambda i, j, k: (i, k))]
    for off in col_offsets:
        in_specs.append(pl.BlockSpec((tk, tn), functools.partial(lambda i, j, k, o: (k, j + o), o=off // tn)))
    out_specs = [pl.BlockSpec((tm, tn), lambda i, j, k: (i, j)) for _ in out_dtypes]
    out_shape = [jax.ShapeDtypeStruct((m, n), dt) for dt in out_dtypes]
    side_args = []
    for w_f32, layer in side:
        _, k2, n2 = w_f32.shape
        n_blocks = _pick(k2 // BF16_ROWS, steps, 1)
        rows = k2 // n_blocks
        blk = functools.partial(lambda i, j, k, nb: jnp.minimum((i * nj + j) * nk + k, nb - 1), nb=n_blocks)
        in_specs.append(pl.BlockSpec((None, rows, n2),
                                     functools.partial(lambda i, j, k, b, l: (l, b(i, j, k), 0), b=blk, l=layer)))
        out_specs.append(pl.BlockSpec((rows, n2), functools.partial(lambda i, j, k, b: (b(i, j, k), 0), b=blk)))
        out_shape.append(jax.ShapeDtypeStruct((k2, n2), BF16))
        side_args.append(w_f32)
    scratch = [pltpu.VMEM((tm, tn), F32) for _ in ws] if nk > 1 else []
    outs = pl.pallas_call(
        functools.partial(_mm_kernel, n_w=n_w, n_side=len(side), n_out=n_out, nk=nk, epilogue=epilogue),
        grid=(m // tm, nj, nk),
        in_specs=in_specs,
        out_specs=out_specs,
        out_shape=out_shape,
        scratch_shapes=scratch,
        compiler_params=_params(("arbitrary", "arbitrary", "arbitrary")),
        name=name,
    )(a, *ws, *side_args)
    return outs


def _ada_kernel(c_ref, w_ref, b_ref, o_ref):
    w = w_ref[...].astype(BF16)
    o_ref[...] = jnp.dot(c_ref[...], w, preferred_element_type=F32) + b_ref[...]


def _adaln(c_act, w_ada, b_ada):
    depth, d, n = w_ada.shape
    rows = c_act.shape[0]
    tn = _pick(n, 512, LANE)
    return pl.pallas_call(
        _ada_kernel,
        grid=(depth, n // tn),
        in_specs=[
            pl.BlockSpec((rows, d), lambda l, j: (0, 0)),
            pl.BlockSpec((None, d, tn), lambda l, j: (l, 0, j)),
            pl.BlockSpec((None, 1, tn), lambda l, j: (l, 0, j)),
        ],
        out_specs=pl.BlockSpec((None, rows, tn), lambda l, j: (l, 0, j)),
        out_shape=jax.ShapeDtypeStruct((depth, rows, n), F32),
        compiler_params=_params(("parallel", "parallel")),
        name="adaln",
    )(c_act, w_ada, b_ada.reshape(depth, 1, n))


class _Slab:
    def __init__(self, batch, seq, dec_batch, dec_seq, d):
        self.batch, self.seq, self.dec_batch, self.dec_seq, self.d = batch, seq, dec_batch, dec_seq, d
        self.m_p = batch * seq
        self.m_s = dec_batch * dec_seq
        self.m = self.m_p + self.m_s
        align = math.lcm(dec_seq * SUBLANE, BF16_ROWS)
        self.tm = _pick(math.gcd(seq, self.m_s), 256, align)
        self.n_p = self.m_p // self.tm
        self.n_t = self.m // self.tm
        self.tiles_per_seq = seq // self.tm
        self.tm_conv = _pick(math.gcd(seq, self.m_s), CONV_TILE_ROWS, align)


def _mod_specs(g, layer, chunk):
    tm, d = g.tm, g.d
    p_spec = pl.BlockSpec((None, 1, d),
                          lambda i: (jnp.minimum(i // g.tiles_per_seq, g.batch - 1), 0, chunk))
    s_spec = pl.BlockSpec((None, tm // g.dec_seq, d), lambda i: (layer, jnp.maximum(i - g.n_p, 0), chunk))
    return p_spec, s_spec


def _seq_to_tokens(v, tm, dec_seq):
    n_seq = tm // dec_seq
    r = lax.broadcasted_iota(jnp.int32, (tm, n_seq), 0) // dec_seq
    c = lax.broadcasted_iota(jnp.int32, (tm, n_seq), 1)
    sel = jnp.where(r == c, 1.0, 0.0).astype(BF16)
    return _dot_exact_lhs(sel, v)


def _modulate_kernel(xp_ref, xs_ref, scp_ref, shp_ref, scs_ref, shs_ref, x_ref, h_ref, *, n_p, dec_seq):
    i = pl.program_id(0)
    tm = x_ref.shape[0]

    @pl.when(i < n_p)
    def _():
        x = xp_ref[...]
        x_ref[...] = x
        h_ref[...] = (x * (1.0 + scp_ref[...]) + shp_ref[...]).astype(h_ref.dtype)

    @pl.when(i >= n_p)
    def _():
        x = xs_ref[...]
        x_ref[...] = x
        sc = _seq_to_tokens(scs_ref[...], tm, dec_seq)
        sh = _seq_to_tokens(shs_ref[...], tm, dec_seq)
        h_ref[...] = (x * (1.0 + sc) + sh).astype(h_ref.dtype)


def _ln_kernel(*refs, n_p, dec_seq, alpha, with_next):
    if with_next:
        (x_ref, o_ref, gp_ref, gs_ref, lg_ref, lb_ref, scp_ref, shp_ref, scs_ref, shs_ref,
         xo_ref, h_ref) = refs
    else:
        x_ref, o_ref, gp_ref, gs_ref, lg_ref, lb_ref, xo_p_ref, xo_s_ref = refs
    i = pl.program_id(0)
    tm = x_ref.shape[0]

    def body(g, sc, sh, out_ref):
        v = alpha * x_ref[...] + g * o_ref[...]
        mu = jnp.mean(v, axis=-1, keepdims=True)
        vc = v - mu
        var = jnp.mean(vc * vc, axis=-1, keepdims=True)
        xn = vc * lax.rsqrt(var + LN_EPS) * lg_ref[...] + lb_ref[...]
        out_ref[...] = xn
        if with_next:
            h_ref[...] = (xn * (1.0 + sc) + sh).astype(h_ref.dtype)

    @pl.when(i < n_p)
    def _():
        body(gp_ref[...], scp_ref[...] if with_next else None, shp_ref[...] if with_next else None,
             xo_ref if with_next else xo_p_ref)

    @pl.when(i >= n_p)
    def _():
        body(_seq_to_tokens(gs_ref[...], tm, dec_seq),
             _seq_to_tokens(scs_ref[...], tm, dec_seq) if with_next else None,
             _seq_to_tokens(shs_ref[...], tm, dec_seq) if with_next else None,
             xo_ref if with_next else xo_s_ref)


def _split_specs(g):
    tm, d = g.tm, g.d
    p_spec = pl.BlockSpec((tm, d), lambda i: (jnp.minimum(i, g.n_p - 1), 0))
    s_spec = pl.BlockSpec((tm, d), lambda i: (jnp.maximum(i - g.n_p, 0), 0))
    return p_spec, s_spec


def _modulate(g, x_p, x_s, mod, mod_p, layer, chunk_sc, chunk_sh):
    tm, d = g.tm, g.d
    scp, scs = _mod_specs(g, layer, chunk_sc)
    shp, shs = _mod_specs(g, layer, chunk_sh)
    xp_spec, xs_spec = _split_specs(g)
    row = pl.BlockSpec((tm, d), lambda i: (i, 0))
    return pl.pallas_call(
        functools.partial(_modulate_kernel, n_p=g.n_p, dec_seq=g.dec_seq),
        grid=(g.n_t,),
        in_specs=[xp_spec, xs_spec, scp, shp, scs, shs],
        out_specs=[row, row],
        out_shape=[jax.ShapeDtypeStruct((g.m, d), F32), jax.ShapeDtypeStruct((g.m, d), BF16)],
        compiler_params=_params(("arbitrary",)),
        name="modulate",
    )(x_p, x_s, mod_p[layer], mod_p[layer], mod, mod)


def _ln_mod(g, x, o, mod, mod_p, layer, chunk_g, ln_g, ln_b, alpha, nxt):
    tm, d = g.tm, g.d
    row = pl.BlockSpec((tm, d), lambda i: (i, 0))
    vec = pl.BlockSpec((1, d), lambda i: (0, 0))
    gp, gs = _mod_specs(g, layer, chunk_g)
    in_specs = [row, row, gp, gs, vec, vec]
    args = [x, o, mod_p[layer], mod, ln_g.reshape(1, d), ln_b.reshape(1, d)]
    if nxt is not None:
        n_layer, c_sc, c_sh = nxt
        scp, scs = _mod_specs(g, n_layer, c_sc)
        shp, shs = _mod_specs(g, n_layer, c_sh)
        in_specs += [scp, shp, scs, shs]
        args += [mod_p[n_layer], mod_p[n_layer], mod, mod]
        out_specs = [row, row]
        out_shape = [jax.ShapeDtypeStruct((g.m, d), F32), jax.ShapeDtypeStruct((g.m, d), BF16)]
    else:
        out_specs = list(_split_specs(g))
        out_shape = [jax.ShapeDtypeStruct((g.m_p, d), F32), jax.ShapeDtypeStruct((g.m_s, d), F32)]
    return pl.pallas_call(
        functools.partial(_ln_kernel, n_p=g.n_p, dec_seq=g.dec_seq, alpha=alpha, with_next=nxt is not None),
        grid=(g.n_t,),
        in_specs=in_specs,
        out_specs=out_specs,
        out_shape=out_shape,
        compiler_params=_params(("arbitrary",)),
        name="ln_mod",
    )(*args)


def _conv_kernel(*refs, n_p, tiles_per_seq, width, dec_seq, with_bias_silu, with_gate):
    u_ref, halo_ref, e_ref, w_ref = refs[:4]
    pos = 4
    b_ref = g_ref = None
    if with_bias_silu:
        b_ref = refs[pos]
        pos += 1
    if with_gate:
        g_ref = refs[pos]
        pos += 1
    o_ref = refs[pos]
    i = pl.program_id(0)
    w = w_ref[...]

    def post(y, rows):
        if with_bias_silu:
            y = _silu(y + b_ref[...])
        if with_gate:
            y = y * g_ref[rows, :].astype(F32)
        return y.astype(o_ref.dtype)

    u = u_ref[...]

    @pl.when(i < n_p)
    def _():
        y = u * w[width - 1:width, :]
        for s in range(1, width):
            y = y + pltpu.roll(u, s, 0) * w[width - 1 - s:width - s, :]
        o_ref[...] = post(y, slice(None))
        keep = jnp.where(i % tiles_per_seq == 0, 0.0, 1.0)
        halo = halo_ref[...] * keep
        halo = jnp.concatenate([halo, halo], axis=0)
        top = u[0:FIX_ROWS, :]
        r = lax.broadcasted_iota(jnp.int32, top.shape, 0)
        yt = top * w[width - 1:width, :]
        for s in range(1, width):
            sh = jnp.where(r < s, pltpu.roll(halo, s, 0), pltpu.roll(top, s, 0))
            yt = yt + sh * w[width - 1 - s:width - s, :]
        o_ref[0:FIX_ROWS, :] = post(yt, slice(0, FIX_ROWS))

    @pl.when(i >= n_p)
    def _():
        tm = u.shape[0]
        t = lax.broadcasted_iota(jnp.int32, u.shape, 0) % dec_seq
        e = e_ref[...]
        y = u * w[width - 1:width, :]
        for s in range(1, width):
            back = width - 1 - s
            src = pltpu.roll(e, tm - back, 0) if back else e
            sh = jnp.where(t >= s, pltpu.roll(u, s, 0), src)
            y = y + sh * w[width - 1 - s:width - s, :]
        o_ref[...] = post(y, slice(None))


def _state_rows(buf, dec_seq):
    n_seq, km1, c = buf.shape
    assert dec_seq >= km1
    return jnp.pad(buf, ((0, 0), (0, dec_seq - km1), (0, 0))).reshape(n_seq * dec_seq, c)


def _causal_conv(g, u, buf_s, w, bias, gate, out_dtype, name):
    m, c = u.shape
    width = w.shape[0]
    tm = g.tm_conv
    tc = _pick(c, CONV_TILE_COLS, LANE)
    n_p = g.m_p // tm
    tile = pl.BlockSpec((tm, tc), lambda i, j: (i, j))
    halo = pl.BlockSpec((SUBLANE, tc), lambda i, j: (jnp.maximum(i * (tm // SUBLANE) - 1, 0), j))
    e_spec = pl.BlockSpec((tm, tc), lambda i, j: (jnp.maximum(i - n_p, 0), j))
    in_specs = [tile, halo, e_spec, pl.BlockSpec((width, tc), lambda i, j: (0, j))]
    args = [u, u, _state_rows(buf_s, g.dec_seq), w]
    if bias is not None:
        in_specs.append(pl.BlockSpec((1, tc), lambda i, j: (0, j)))
        args.append(bias.reshape(1, c))
    if gate is not None:
        in_specs.append(tile)
        args.append(gate)
    return pl.pallas_call(
        functools.partial(_conv_kernel, n_p=n_p, tiles_per_seq=g.seq // tm, width=width,
                          dec_seq=g.dec_seq, with_bias_silu=bias is not None, with_gate=gate is not None),
        grid=(m // tm, c // tc),
        in_specs=in_specs,
        out_specs=tile,
        out_shape=jax.ShapeDtypeStruct((m, c), out_dtype),
        compiler_params=_params(("parallel", "parallel")),
        name=name,
    )(*args)


def _dt_kernel(r_ref, b_ref, al_ref, dt_ref, adt_ref):
    dt = jax.nn.softplus(r_ref[...] + b_ref[...])
    dt_ref[...] = dt
    adt_ref[...] = dt * (-jnp.exp(al_ref[...]))


def _dt_prep(dt_raw, dt_bias, a_log):
    m, h = dt_raw.shape
    tm = _pick(m, 1024, SUBLANE)
    row = pl.BlockSpec((tm, h), lambda i: (i, 0))
    vec = pl.BlockSpec((1, h), lambda i: (0, 0))
    return pl.pallas_call(
        _dt_kernel,
        grid=(m // tm,),
        in_specs=[row, vec, vec],
        out_specs=[row, row],
        out_shape=[jax.ShapeDtypeStruct((m, h), F32)] * 2,
        compiler_params=_params(("parallel",)),
        name="ssd_dt",
    )(dt_raw, dt_bias.reshape(1, h), a_log.reshape(1, h))


def _ssd_kernel(*refs, q, n_sub, g_blk, r_heads, p_dim, zero_init, n_alias):
    (xs_ref, b_ref, c_ref, z_ref, dtc_ref, adtc_ref, adtr_ref, dsk_ref, ng_ref, ex_ref) = refs[:10]
    pos = 10
    s0_ref = None
    if not zero_init:
        s0_ref = refs[pos]
        pos += 1
    pos += n_alias
    y_ref, s_ref = refs[pos], refs[pos + 1]
    ci = pl.program_id(2)

    @pl.when(ci == 0)
    def _():
        if zero_init:
            s_ref[...] = jnp.zeros_like(s_ref)
        else:
            s_ref[...] = s0_ref[...]

    rows = n_sub * q
    row = lax.broadcasted_iota(jnp.int32, (rows, rows), 0)
    col = lax.broadcasted_iota(jnp.int32, (rows, rows), 1)
    if n_sub == 1:
        causal = row >= col
        anti = row <= col
    else:
        same = (row // q) == (col // q)
        causal = jnp.logical_and(same, row >= col)
        anti = jnp.logical_and(same, row <= col)
    tril = jnp.where(causal, 1.0, 0.0).astype(BF16)
    triu = jnp.where(anti, 1.0, 0.0).astype(BF16)
    for gi in range(g_blk):
        _ssd_group(gi, causal, tril, triu, xs_ref, b_ref, c_ref, z_ref, dtc_ref, adtc_ref, adtr_ref, dsk_ref,
                   ng_ref, ex_ref, y_ref, s_ref, q=q, n_sub=n_sub, r_heads=r_heads, p_dim=p_dim)


def _ssd_group(gi, causal, tril, triu, xs_ref, b_ref, c_ref, z_ref, dtc_ref, adtc_ref, adtr_ref, dsk_ref,
               ng_ref, ex_ref, y_ref, s_ref, *, q, n_sub, r_heads, p_dim):
    rows = n_sub * q
    n_state = s_ref.shape[-1]
    rp = r_heads * p_dim
    ch = slice(gi * rp, (gi + 1) * rp)
    st = slice(gi * n_state, (gi + 1) * n_state)
    xs = xs_ref[:, ch]
    b_f = b_ref[:, st]
    c_f = c_ref[:, st]
    bm = b_f.astype(BF16)
    cm = c_f.astype(BF16)
    dt_c = dtc_ref[gi]
    adt_c = adtc_ref[gi]
    adt_r = adtr_ref[gi]
    cs_c = _dot_exact_lhs(tril, adt_c)
    cs_r = _dot_exact_rhs(adt_r, triu)
    lasts = [cs_c[k * q + q - 1:k * q + q, :] for k in range(n_sub)]
    if n_sub == 1:
        cs_last = lasts[0]
    else:
        cs_last = jnp.concatenate([jnp.broadcast_to(v, (q, r_heads)) for v in lasts], axis=0)

    stacked = jnp.concatenate([dt_c, jnp.exp(cs_c), jnp.exp(cs_last - cs_c)], axis=0)
    wide = _dot_exact_rhs(stacked, ex_ref[...])
    dt_x, ecs_x, dec_x = wide[0:rows], wide[rows:2 * rows], wide[2 * rows:3 * rows]

    xdt = xs * dt_x
    xb = xdt.astype(BF16)
    xdec = xdt * dec_x

    y_state = []
    for k in range(n_sub):
        seq = slice(k * q, (k + 1) * q)
        s_old = s_ref[k, gi]
        y_state.append(lax.dot_general(c_f[seq].astype(BF16), s_old.astype(BF16), (((1,), (1,)), ((), ())),
                                       preferred_element_type=F32))
        upd = lax.dot_general(xdec[seq].astype(BF16), b_f[seq].astype(BF16), (((0,), (0,)), ((), ())),
                              preferred_element_type=F32)
        e_heads = jnp.broadcast_to(jnp.exp(cs_r[:, k * q + q - 1:k * q + q]), (r_heads, n_state))
        for h in range(r_heads):
            hp = slice(h * p_dim, (h + 1) * p_dim)
            s_ref[k, gi, hp, :] = s_old[hp] * e_heads[h:h + 1, :] + upd[hp]
    y = (y_state[0] if n_sub == 1 else jnp.concatenate(y_state, axis=0)) * ecs_x

    cb = lax.dot_general(cm, bm, (((1,), (1,)), ((), ())), preferred_element_type=F32)
    pieces = []
    heads_per_blk = max(1, LANE // p_dim)
    assert r_heads % heads_per_blk == 0
    for blk in range(r_heads // heads_per_blk):
        width = heads_per_blk * p_dim
        xblk = xb[:, blk * width:(blk + 1) * width]
        lane = lax.broadcasted_iota(jnp.int32, (rows, width), 1)
        acc = None
        for k in range(heads_per_blk):
            h = blk * heads_per_blk + k
            diff = cs_c[:, h:h + 1] - cs_r[h:h + 1, :]
            lmat = jnp.exp(jnp.where(causal, diff, -jnp.inf))
            yk = jnp.dot((cb * lmat).astype(BF16), xblk, preferred_element_type=F32)
            acc = yk if acc is None else jnp.where(lane >= k * p_dim, yk, acc)
        pieces.append(acc)
    y = y + (pieces[0] if len(pieces) == 1 else jnp.concatenate(pieces, axis=1))

    y = y + dsk_ref[:, ch] * xs
    yz = y * _silu(z_ref[:, ch].astype(F32))
    ms = jnp.mean(yz * yz, axis=-1, keepdims=True)
    y_ref[:, ch] = (yz * lax.rsqrt(ms + RMS_EPS) * ng_ref[:, ch]).astype(y_ref.dtype)


def _ssd_scan(xbc, z, dt, adt, d_skip_x, norm_g, s0, layer, y_buf, s_buf, *, m, n_layers, row0, n_seq,
              seq_len, n_sub, g_blk, groups, r_heads, p_dim, n_state):
    q = SSD_CHUNK if seq_len % SSD_CHUNK == 0 else seq_len
    nc = seq_len // q
    rp = r_heads * p_dim
    d_inner = groups * rp
    rows_blk = n_sub * q
    assert nc == 1 or n_sub == 1
    assert row0 % rows_blk == 0 and n_seq % n_sub == 0 and rows_blk % BF16_ROWS == 0
    assert rp % LANE == 0 and n_state % LANE == 0 and d_inner % n_state == 0
    blk0 = row0 // rows_blk
    rows = n_seq * seq_len

    def per_group(v):
        v = v[row0:row0 + rows].reshape(rows // rows_blk, rows_blk, groups, r_heads)
        return v.transpose(2, 0, 1, 3), v.transpose(2, 0, 3, 1)

    dt_c, _ = per_group(dt)
    adt_c, adt_r = per_group(adt)
    head_of = np.arange(rp) // p_dim
    expand = jnp.asarray(head_of[None, :] == np.arange(r_heads)[:, None], dtype=BF16)

    assert groups % g_blk == 0 and (d_inner // n_state) % g_blk == 0
    b_col0 = d_inner // (g_blk * n_state)
    c_col0 = b_col0 + groups // g_blk
    tok = lambda s, g, c: (blk0 + s * nc + c, g)
    per_chunk = lambda s, g, c: (g, s * nc + c, 0, 0)
    const = lambda s, g, c: (0, 0)
    in_specs = [
        pl.BlockSpec((rows_blk, g_blk * rp), tok),
        pl.BlockSpec((rows_blk, g_blk * n_state), lambda s, g, c: (blk0 + s * nc + c, b_col0 + g)),
        pl.BlockSpec((rows_blk, g_blk * n_state), lambda s, g, c: (blk0 + s * nc + c, c_col0 + g)),
        pl.BlockSpec((rows_blk, g_blk * rp), tok),
        pl.BlockSpec((g_blk, None, rows_blk, r_heads), per_chunk),
        pl.BlockSpec((g_blk, None, rows_blk, r_heads), per_chunk),
        pl.BlockSpec((g_blk, None, r_heads, rows_blk), per_chunk),
        pl.BlockSpec((1, g_blk * rp), lambda s, g, c: (0, g)),
        pl.BlockSpec((1, g_blk * rp), lambda s, g, c: (0, g)),
        pl.BlockSpec((r_heads, rp), const),
    ]
    args = [xbc, xbc, xbc, z, dt_c, adt_c, adt_r, d_skip_x, norm_g.reshape(1, d_inner), expand]
    state_spec = pl.BlockSpec((None, n_sub, g_blk, rp, n_state), lambda s, g, c: (layer, s, g, 0, 0))
    if s0 is not None:
        in_specs.append(state_spec)
        args.append(s0.reshape(n_layers, n_seq, groups, rp, n_state))
    aliases = {}
    for out_idx, buf in enumerate((y_buf, s_buf)):
        if buf is not None:
            aliases[len(args)] = out_idx
            in_specs.append(pl.BlockSpec(memory_space=pl.ANY))
            args.append(buf)
    y, s_new = pl.pallas_call(
        functools.partial(_ssd_kernel, q=q, n_sub=n_sub, g_blk=g_blk, r_heads=r_heads, p_dim=p_dim,
                          zero_init=s0 is None, n_alias=len(aliases)),
        grid=(n_seq // n_sub, groups // g_blk, nc),
        in_specs=in_specs,
        out_specs=[pl.BlockSpec((rows_blk, g_blk * rp), tok), state_spec],
        out_shape=[jax.ShapeDtypeStruct((m, d_inner), BF16),
                   jax.ShapeDtypeStruct((n_layers, n_seq, groups, rp, n_state), F32)],
        input_output_aliases=aliases,
        compiler_params=_params(("parallel", "parallel", "arbitrary")),
        name="ssd_scan_q%d" % q,
    )(*args)
    return y, s_new


def _tails(u, g, k):
    c = u.shape[1]
    last = np.arange(-k, 0)
    rows_p = ((np.arange(g.batch)[:, None] + 1) * g.seq + last[None, :]).reshape(-1)
    rows_s = (g.m_p + (np.arange(g.dec_batch)[:, None] + 1) * g.dec_seq + last[None, :]).reshape(-1)
    p = jnp.take(u, jnp.asarray(rows_p, jnp.int32), axis=0).reshape(g.batch, k, c)
    s = jnp.take(u, jnp.asarray(rows_s, jnp.int32), axis=0).reshape(g.dec_batch, k, c)
    return p, s


def kernel(x_prompt, x_sample, state_sconv, state_ssd_conv, state_ssd, c_prompt, c_sample, w_ada, b_ada, ln_g, ln_b, sc_w_in, sc_conv_w, sc_w_out, ssd_w_in, ssd_conv_w, ssd_conv_b, ssd_dt_bias, ssd_a_log, ssd_d, ssd_norm_g, ssd_w_out, ffn_w_gate, ffn_w_up, ffn_w_down):
    batch, seq, d = x_prompt.shape
    dec_batch, dec_seq, _ = x_sample.shape
    depth = w_ada.shape[0]
    g = _Slab(batch, seq, dec_batch, dec_seq, d)
    alpha = (2 * depth) ** 0.25

    n_ssd, _, heads, p_dim, n_state = state_ssd.shape
    d_inner = heads * p_dim
    conv_dim = ssd_conv_w.shape[-1]
    groups = (conv_dim - d_inner) // (2 * n_state)
    r_heads = heads // groups
    ffn_hidden = ffn_w_gate.shape[-1]

    n_c = batch + dec_batch
    n_c_pad = -(-n_c // BF16_ROWS) * BF16_ROWS
    c_all = jnp.concatenate([c_sample, c_prompt], axis=0)
    c_act = jnp.pad(_silu(c_all), ((0, n_c_pad - n_c), (0, 0))).astype(BF16)
    mod = _adaln(c_act, w_ada, b_ada)
    mod_p = [mod[l, dec_batch:n_c].reshape(batch, 1, N_MOD * d) for l in range(depth)]

    mix_in_b, mix_out_b = sc_w_in[0].astype(BF16), sc_w_out[0].astype(BF16)
    w_gate_b, w_up_b = ffn_w_gate[0].astype(BF16), ffn_w_up[0].astype(BF16)

    x, h = _modulate(g, x_prompt.reshape(g.m_p, d), x_sample.reshape(g.m_s, d), mod, mod_p, 0, 1, 0)

    sc_p, sc_s, cv_p, cv_s = [], [], [], []
    st_p = st_s = None
    for i in range(depth):
        j = i // 2
        if i % 2 == 0:
            gate_b, u = _matmul(h, [mix_in_b] * 3, [0, d, 2 * d], d, [BF16, F32],
                                epilogue="sconv", tm=1024, tn=512, tk=d, name="sconv_in")
            y = _causal_conv(g, u, state_sconv[j], sc_conv_w[j], None, gate_b, BF16, "sconv_conv")
            tp, ts = _tails(u, g, sc_conv_w.shape[1] - 1)
            sc_p.append(tp)
            sc_s.append(ts)
            (out,) = _matmul(y, [mix_out_b], [0], d, [BF16],
                             epilogue="plain", tm=1024, tn=1024, tk=4096, name="sconv_out")
        else:
            (z,) = _matmul(h, [mix_in_b], [0], d_inner, [BF16],
                           epilogue="plain", tm=1024, tn=1024, tk=4096, name="ssd_in_z")
            (xbc,) = _matmul(h, [mix_in_b], [d_inner], conv_dim, [F32],
                             epilogue="plain", tm=1024, tn=1024, tk=4096, name="ssd_in_xbc")
            (dt_raw,) = _matmul(h, [mix_in_b], [d_inner + conv_dim], heads, [F32],
                                epilogue="plain", tm=1024, tn=heads, tk=4096, name="ssd_in_dt")
            tp, ts = _tails(xbc, g, ssd_conv_w.shape[1] - 1)
            cv_p.append(tp)
            cv_s.append(ts)
            xbc_act = _causal_conv(g, xbc, state_ssd_conv[j], ssd_conv_w[j], ssd_conv_b[j], None, F32, "ssd_conv")
            dt, adt = _dt_prep(dt_raw, ssd_dt_bias[j], ssd_a_log[j])
            d_skip_x = jnp.repeat(ssd_d[j], p_dim).reshape(1, d_inner)
            common = dict(m=g.m, n_layers=n_ssd, groups=groups, r_heads=r_heads, p_dim=p_dim, n_state=n_state)
            y, st_p = _ssd_scan(xbc_act, z, dt, adt, d_skip_x, ssd_norm_g[j], None, j, None, st_p,
                                row0=0, n_seq=batch, seq_len=seq, n_sub=1,
                                g_blk=_pick(groups, SSD_PROMPT_GROUPS_PER_STEP, 1), **common)
            n_sub = _pick(dec_batch, max(1, SSD_CHUNK // dec_seq), 1) if dec_seq % SSD_CHUNK else 1
            y, st_s = _ssd_scan(xbc_act, z, dt, adt, d_skip_x, ssd_norm_g[j], state_ssd, j, y, st_s,
                                row0=g.m_p, n_seq=dec_batch, seq_len=dec_seq, n_sub=n_sub, g_blk=1, **common)
            (out,) = _matmul(y, [mix_out_b], [0], d, [BF16],
                             epilogue="plain", tm=1024, tn=1024, tk=4096, name="ssd_out")
        x, h = _ln_mod(g, x, out, mod, mod_p, i, 2, ln_g[i, 0], ln_b[i, 0], alpha, (i, 4, 3))

        more = i + 1 < depth
        side = [(ffn_w_down, i)] + ([(ffn_w_gate, i + 1), (ffn_w_up, i + 1)] if more else [])
        hid, w_down_b, *nxt_ffn = _matmul(h, [w_gate_b, w_up_b], [0, 0], ffn_hidden, [BF16], side=side,
                                          epilogue="swiglu", tm=1024, tn=512, tk=d, name="ffn_in",
                                          ragged_n=True)
        if more:
            stacks = (ssd_w_in, ssd_w_out) if (i + 1) % 2 else (sc_w_in, sc_w_out)
            side = [(w, (i + 1) // 2) for w in stacks]
        else:
            side = []
        out, *nxt_mix = _matmul(hid, [w_down_b], [0], d, [BF16], side=side,
                                epilogue="plain", tm=1024, tn=512, tk=5632, name="ffn_out")
        if more:
            w_gate_b, w_up_b = nxt_ffn
            mix_in_b, mix_out_b = nxt_mix
        nxt = (i + 1, 1, 0) if more else None
        x, h = _ln_mod(g, x, out, mod, mod_p, i, 5, ln_g[i, 1], ln_b[i, 1], alpha, nxt)

    rows_prompt, rows_sample = x, h
    y_prompt = rows_prompt.reshape(batch, seq, d)
    y_sample = rows_sample.reshape(dec_batch, dec_seq, d)
    st_p = st_p.reshape(n_ssd, batch, heads, p_dim, n_state)
    st_s = st_s.reshape(n_ssd, dec_batch, heads, p_dim, n_state)
    return (y_prompt, y_sample, jnp.stack(sc_p), jnp.stack(cv_p), st_p,
            jnp.stack(sc_s), jnp.stack(cv_s), st_s)
```

```python
import functools
import math

import numpy as np
import jax
import jax.numpy as jnp
from jax import lax
from jax.experimental import pallas as pl
from jax.experimental.pallas import tpu as pltpu

F32 = jnp.float32
BF16 = jnp.bfloat16

LN_EPS = 1e-5
RMS_EPS = 1e-5
SSD_CHUNK = 128
N_MOD = 6

V7X_VMEM_LIMIT_BYTES = 56 * 1024 * 1024
LANE = 128
SUBLANE = 8
BF16_ROWS = 2 * SUBLANE
FIX_ROWS = BF16_ROWS
SSD_PROMPT_GROUPS_PER_STEP = 4
CONV_TILE_ROWS = 256
CONV_TILE_COLS = 5120


def _pick(dim, pref, align):
    best = None
    t = align
    while t <= min(dim, pref):
        if dim % t == 0:
            best = t
        t += align
    return best if best is not None else dim


def _params(sem):
    return pltpu.CompilerParams(dimension_semantics=sem, vmem_limit_bytes=V7X_VMEM_LIMIT_BYTES)


def _silu(x):
    return x * jax.nn.sigmoid(x)


def _split3(v):
    hi = v.astype(BF16)
    r1 = v - hi.astype(F32)
    mid = r1.astype(BF16)
    lo = (r1 - mid.astype(F32)).astype(BF16)
    return hi, mid, lo


def _dot_exact_lhs(sel, v):
    return sum(jnp.dot(sel, p, preferred_element_type=F32) for p in _split3(v))


def _dot_exact_rhs(v, sel):
    return sum(jnp.dot(p, sel, preferred_element_type=F32) for p in _split3(v))


def _mm_kernel(*refs, n_w, n_side, n_out, nk, epilogue):
    a_ref = refs[0]
    w_refs = refs[1:1 + n_w]
    pos = 1 + n_w
    side_in = refs[pos:pos + n_side]
    pos += n_side
    o_refs = refs[pos:pos + n_out]
    pos += n_out
    side_out = refs[pos:pos + n_side]
    acc_refs = refs[pos + n_side:]

    for src, dst in zip(side_in, side_out):
        dst[...] = src[...].astype(dst.dtype)

    def finish(accs):
        if epilogue == "plain":
            o_refs[0][...] = accs[0].astype(o_refs[0].dtype)
        elif epilogue == "swiglu":
            o_refs[0][...] = (_silu(accs[0]) * accs[1]).astype(o_refs[0].dtype)
        else:
            o_refs[0][...] = accs[0].astype(o_refs[0].dtype)
            o_refs[1][...] = (accs[1] * accs[2]).astype(o_refs[1].dtype)

    a = a_ref[...]
    parts = [jnp.dot(a, w[...], preferred_element_type=F32) for w in w_refs]
    if nk == 1:
        finish(parts)
        return
    k = pl.program_id(2)

    @pl.when(k == 0)
    def _():
        for acc, p in zip(acc_refs, parts):
            acc[...] = p

    @pl.when(k > 0)
    def _():
        for acc, p in zip(acc_refs, parts):
            acc[...] += p

    @pl.when(k == nk - 1)
    def _():
        finish([acc[...] for acc in acc_refs])


def _matmul(a, ws, col_offsets, n, out_dtypes, *, epilogue, tm, tn, tk, name, ragged_n=False, side=()):
    m, kdim = a.shape
    tm = _pick(m, tm, BF16_ROWS)
    tn = min(tn, n) if ragged_n else _pick(n, tn, LANE)
    tk = _pick(kdim, tk, LANE)
    nk = kdim // tk
    for off in col_offsets:
        assert off % tn == 0
    n_w = len(ws)
    n_out = len(out_dtypes)
    nj = pl.cdiv(n, tn)
    steps = (m // tm) * nj * nk
    in_specs = [pl.BlockSpec((tm, tk), lambda i, j, k: (i, k))]
    for off in col_offsets:
        in_specs.append(pl.BlockSpec((tk, tn), functools.partial(lambda i, j, k, o: (k, j + o), o=off // tn)))
    out_specs = [pl.BlockSpec((tm, tn), lambda i, j, k: (i, j)) for _ in out_dtypes]
    out_shape = [jax.ShapeDtypeStruct((m, n), dt) for dt in out_dtypes]
    side_args = []
    for w_f32, layer in side:
        _, k2, n2 = w_f32.shape
        n_blocks = _pick(k2 // BF16_ROWS, steps, 1)
        rows = k2 // n_blocks
        blk = functools.partial(lambda i, j, k, nb: jnp.minimum((i * nj + j) * nk + k, nb - 1), nb=n_blocks)
        in_specs.append(pl.BlockSpec((None, rows, n2),
                                     functools.partial(lambda i, j, k, b, l: (l, b(i, j, k), 0), b=blk, l=layer)))
        out_specs.append(pl.BlockSpec((rows, n2), functools.partial(lambda i, j, k, b: (b(i, j, k), 0), b=blk)))
        out_shape.append(jax.ShapeDtypeStruct((k2, n2), BF16))
        side_args.append(w_f32)
    scratch = [pltpu.VMEM((tm, tn), F32) for _ in ws] if nk > 1 else []
    outs = pl.pallas_call(
        functools.partial(_mm_kernel, n_w=n_w, n_side=len(side), n_out=n_out, nk=nk, epilogue=epilogue),
        grid=(m // tm, nj, nk),
        in_specs=in_specs,
        out_specs=out_specs,
        out_shape=out_shape,
        scratch_shapes=scratch,
        compiler_params=_params(("arbitrary", "arbitrary", "arbitrary")),
        name=name,
    )(a, *ws, *side_args)
    return outs


def _ada_kernel(c_ref, w_ref, b_ref, o_ref):
    w = w_ref[...].astype(BF16)
    o_ref[...] = jnp.dot(c_ref[...], w, preferred_element_type=F32) + b_ref[...]


def _adaln(c_act, w_ada, b_ada):
    depth, d, n = w_ada.shape
    rows = c_act.shape[0]
    tn = _pick(n, 512, LANE)
    return pl.pallas_call(
        _ada_kernel,
        grid=(depth, n // tn),
        in_specs=[
            pl.BlockSpec((rows, d), lambda l, j: (0, 0)),
            pl.BlockSpec((None, d, tn), lambda l, j: (l, 0, j)),
            pl.BlockSpec((None, 1, tn), lambda l, j: (l, 0, j)),
        ],
        out_specs=pl.BlockSpec((None, rows, tn), lambda l, j: (l, 0, j)),
        out_shape=jax.ShapeDtypeStruct((depth, rows, n), F32),
        compiler_params=_params(("parallel", "parallel")),
        name="adaln",
    )(c_act, w_ada, b_ada.reshape(depth, 1, n))


class _Slab:
    def __init__(self, batch, seq, dec_batch, dec_seq, d):
        self.batch, self.seq, self.dec_batch, self.dec_seq, self.d = batch, seq, dec_batch, dec_seq, d
        self.m_p = batch * seq
        self.m_s = dec_batch * dec_seq
        self.m = self.m_p + self.m_s
        align = math.lcm(dec_seq * SUBLANE, BF16_ROWS)
        self.tm = _pick(math.gcd(seq, self.m_s), 256, align)
        self.n_p = self.m_p // self.tm
        self.n_t = self.m // self.tm
        self.tiles_per_seq = seq // self.tm
        self.tm_conv = _pick(math.gcd(seq, self.m_s), CONV_TILE_ROWS, align)


def _mod_specs(g, layer, chunk):
    tm, d = g.tm, g.d
    p_spec = pl.BlockSpec((None, 1, d),
                          lambda i: (jnp.minimum(i // g.tiles_per_seq, g.batch - 1), 0, chunk))
    s_spec = pl.BlockSpec((None, tm // g.dec_seq, d), lambda i: (layer, jnp.maximum(i - g.n_p, 0), chunk))
    return p_spec, s_spec


def _seq_to_tokens(v, tm, dec_seq):
    n_seq = tm // dec_seq
    r = lax.broadcasted_iota(jnp.int32, (tm, n_seq), 0) // dec_seq
    c = lax.broadcasted_iota(jnp.int32, (tm, n_seq), 1)
    sel = jnp.where(r == c, 1.0, 0.0).astype(BF16)
    return _dot_exact_lhs(sel, v)


def _modulate_kernel(xp_ref, xs_ref, scp_ref, shp_ref, scs_ref, shs_ref, x_ref, h_ref, *, n_p, dec_seq):
    i = pl.program_id(0)
    tm = x_ref.shape[0]

    @pl.when(i < n_p)
    def _():
        x = xp_ref[...]
        x_ref[...] = x
        h_ref[...] = (x * (1.0 + scp_ref[...]) + shp_ref[...]).astype(h_ref.dtype)

    @pl.when(i >= n_p)
    def _():
        x = xs_ref[...]
        x_ref[...] = x
        sc = _seq_to_tokens(scs_ref[...], tm, dec_seq)
        sh = _seq_to_tokens(shs_ref[...], tm, dec_seq)
        h_ref[...] = (x * (1.0 + sc) + sh).astype(h_ref.dtype)


def _ln_kernel(*refs, n_p, dec_seq, alpha, with_next):
    if with_next:
        (x_ref, o_ref, gp_ref, gs_ref, lg_ref, lb_ref, scp_ref, shp_ref, scs_ref, shs_ref,
         xo_ref, h_ref) = refs
    else:
        x_ref, o_ref, gp_ref, gs_ref, lg_ref, lb_ref, xo_p_ref, xo_s_ref = refs
    i = pl.program_id(0)
    tm = x_ref.shape[0]

    def body(g, sc, sh, out_ref):
        v = alpha * x_ref[...] + g * o_ref[...]
        mu = jnp.mean(v, axis=-1, keepdims=True)
        vc = v - mu
        var = jnp.mean(vc * vc, axis=-1, keepdims=True)
        xn = vc * lax.rsqrt(var + LN_EPS) * lg_ref[...] + lb_ref[...]
        out_ref[...] = xn
        if with_next:
            h_ref[...] = (xn * (1.0 + sc) + sh).astype(h_ref.dtype)

    @pl.when(i < n_p)
    def _():
        body(gp_ref[...], scp_ref[...] if with_next else None, shp_ref[...] if with_next else None,
             xo_ref if with_next else xo_p_ref)

    @pl.when(i >= n_p)
    def _():
        body(_seq_to_tokens(gs_ref[...], tm, dec_seq),
             _seq_to_tokens(scs_ref[...], tm, dec_seq) if with_next else None,
             _seq_to_tokens(shs_ref[...], tm, dec_seq) if with_next else None,
             xo_ref if with_next else xo_s_ref)


def _split_specs(g):
    tm, d = g.tm, g.d
    p_spec = pl.BlockSpec((tm, d), lambda i: (jnp.minimum(i, g.n_p - 1), 0))
    s_spec = pl.BlockSpec((tm, d), lambda i: (jnp.maximum(i - g.n_p, 0), 0))
    return p_spec, s_spec


def _modulate(g, x_p, x_s, mod, mod_p, layer, chunk_sc, chunk_sh):
    tm, d = g.tm, g.d
    scp, scs = _mod_specs(g, layer, chunk_sc)
    shp, shs = _mod_specs(g, layer, chunk_sh)
    xp_spec, xs_spec = _split_specs(g)
    row = pl.BlockSpec((tm, d), lambda i: (i, 0))
    return pl.pallas_call(
        functools.partial(_modulate_kernel, n_p=g.n_p, dec_seq=g.dec_seq),
        grid=(g.n_t,),
        in_specs=[xp_spec, xs_spec, scp, shp, scs, shs],
        out_specs=[row, row],
        out_shape=[jax.ShapeDtypeStruct((g.m, d), F32), jax.ShapeDtypeStruct((g.m, d), BF16)],
        compiler_params=_params(("arbitrary",)),
        name="modulate",
    )(x_p, x_s, mod_p[layer], mod_p[layer], mod, mod)


def _ln_mod(g, x, o, mod, mod_p, layer, chunk_g, ln_g, ln_b, alpha, nxt):
    tm, d = g.tm, g.d
    row = pl.BlockSpec((tm, d), lambda i: (i, 0))
    vec = pl.BlockSpec((1, d), lambda i: (0, 0))
    gp, gs = _mod_specs(g, layer, chunk_g)
    in_specs = [row, row, gp, gs, vec, vec]
    args = [x, o, mod_p[layer], mod, ln_g.reshape(1, d), ln_b.reshape(1, d)]
    if nxt is not None:
        n_layer, c_sc, c_sh = nxt
        scp, scs = _mod_specs(g, n_layer, c_sc)
        shp, shs = _mod_specs(g, n_layer, c_sh)
        in_specs += [scp, shp, scs, shs]
        args += [mod_p[n_layer], mod_p[n_layer], mod, mod]
        out_specs = [row, row]
        out_shape = [jax.ShapeDtypeStruct((g.m, d), F32), jax.ShapeDtypeStruct((g.m, d), BF16)]
    else:
        out_specs = list(_split_specs(g))
        out_shape = [jax.ShapeDtypeStruct((g.m_p, d), F32), jax.ShapeDtypeStruct((g.m_s, d), F32)]
    return pl.pallas_call(
        functools.partial(_ln_kernel, n_p=g.n_p, dec_seq=g.dec_seq, alpha=alpha, with_next=nxt is not None),
        grid=(g.n_t,),
        in_specs=in_specs,
        out_specs=out_specs,
        out_shape=out_shape,
        compiler_params=_params(("arbitrary",)),
        name="ln_mod",
    )(*args)


def _conv_kernel(*refs, n_p, tiles_per_seq, width, dec_seq, with_bias_silu, with_gate):
    u_ref, halo_ref, e_ref, w_ref = refs[:4]
    pos = 4
    b_ref = g_ref = None
    if with_bias_silu:
        b_ref = refs[pos]
        pos += 1
    if with_gate:
        g_ref = refs[pos]
        pos += 1
    o_ref = refs[pos]
    i = pl.program_id(0)
    w = w_ref[...]

    def post(y, rows):
        if with_bias_silu:
            y = _silu(y + b_ref[...])
        if with_gate:
            y = y * g_ref[rows, :].astype(F32)
        return y.astype(o_ref.dtype)

    u = u_ref[...]

    @pl.when(i < n_p)
    def _():
        y = u * w[width - 1:width, :]
        for s in range(1, width):
            y = y + pltpu.roll(u, s, 0) * w[width - 1 - s:width - s, :]
        o_ref[...] = post(y, slice(None))
        keep = jnp.where(i % tiles_per_seq == 0, 0.0, 1.0)
        halo = halo_ref[...] * keep
        halo = jnp.concatenate([halo, halo], axis=0)
        top = u[0:FIX_ROWS, :]
        r = lax.broadcasted_iota(jnp.int32, top.shape, 0)
        yt = top * w[width - 1:width, :]
        for s in range(1, width):
            sh = jnp.where(r < s, pltpu.roll(halo, s, 0), pltpu.roll(top, s, 0))
            yt = yt + sh * w[width - 1 - s:width - s, :]
        o_ref[0:FIX_ROWS, :] = post(yt, slice(0, FIX_ROWS))

    @pl.when(i >= n_p)
    def _():
        tm = u.shape[0]
        t = lax.broadcasted_iota(jnp.int32, u.shape, 0) % dec_seq
        e = e_ref[...]
        y = u * w[width - 1:width, :]
        for s in range(1, width):
            back = width - 1 - s
            src = pltpu.roll(e, tm - back, 0) if back else e
            sh = jnp.where(t >= s, pltpu.roll(u, s, 0), src)
            y = y + sh * w[width - 1 - s:width - s, :]
        o_ref[...] = post(y, slice(None))


def _state_rows(buf, dec_seq):
    n_seq, km1, c = buf.shape
    assert dec_seq >= km1
    return jnp.pad(buf, ((0, 0), (0, dec_seq - km1), (0, 0))).reshape(n_seq * dec_seq, c)


def _causal_conv(g, u, buf_s, w, bias, gate, out_dtype, name):
    m, c = u.shape
    width = w.shape[0]
    tm = g.tm_conv
    tc = _pick(c, CONV_TILE_COLS, LANE)
    n_p = g.m_p // tm
    tile = pl.BlockSpec((tm, tc), lambda i, j: (i, j))
    halo = pl.BlockSpec((SUBLANE, tc), lambda i, j: (jnp.maximum(i * (tm // SUBLANE) - 1, 0), j))
    e_spec = pl.BlockSpec((tm, tc), lambda i, j: (jnp.maximum(i - n_p, 0), j))
    in_specs = [tile, halo, e_spec, pl.BlockSpec((width, tc), lambda i, j: (0, j))]
    args = [u, u, _state_rows(buf_s, g.dec_seq), w]
    if bias is not None:
        in_specs.append(pl.BlockSpec((1, tc), lambda i, j: (0, j)))
        args.append(bias.reshape(1, c))
    if gate is not None:
        in_specs.append(tile)
        args.append(gate)
    return pl.pallas_call(
        functools.partial(_conv_kernel, n_p=n_p, tiles_per_seq=g.seq // tm, width=width,
                          dec_seq=g.dec_seq, with_bias_silu=bias is not None, with_gate=gate is not None),
        grid=(m // tm, c // tc),
        in_specs=in_specs,
        out_specs=tile,
        out_shape=jax.ShapeDtypeStruct((m, c), out_dtype),
        compiler_params=_params(("parallel", "parallel")),
        name=name,
    )(*args)


def _dt_kernel(r_ref, b_ref, al_ref, dt_ref, adt_ref):
    dt = jax.nn.softplus(r_ref[...] + b_ref[...])
    dt_ref[...] = dt
    adt_ref[...] = dt * (-jnp.exp(al_ref[...]))


def _dt_prep(dt_raw, dt_bias, a_log):
    m, h = dt_raw.shape
    tm = _pick(m, 1024, SUBLANE)
    row = pl.BlockSpec((tm, h), lambda i: (i, 0))
    vec = pl.BlockSpec((1, h), lambda i: (0, 0))
    return pl.pallas_call(
        _dt_kernel,
        grid=(m // tm,),
        in_specs=[row, vec, vec],
        out_specs=[row, row],
        out_shape=[jax.ShapeDtypeStruct((m, h), F32)] * 2,
        compiler_params=_params(("parallel",)),
        name="ssd_dt",
    )(dt_raw, dt_bias.reshape(1, h), a_log.reshape(1, h))


def _ssd_kernel(*refs, q, n_sub, g_blk, r_heads, p_dim, zero_init, n_alias):
    (xs_ref, b_ref, c_ref, z_ref, dtc_ref, adtc_ref, adtr_ref, dsk_ref, ng_ref, ex_ref) = refs[:10]
    pos = 10
    s0_ref = None
    if not zero_init:
        s0_ref = refs[pos]
        pos += 1
    pos += n_alias
    y_ref, s_ref = refs[pos], refs[pos + 1]
    ci = pl.program_id(2)

    @pl.when(ci == 0)
    def _():
        if zero_init:
            s_ref[...] = jnp.zeros_like(s_ref)
        else:
            s_ref[...] = s0_ref[...]

    rows = n_sub * q
    row = lax.broadcasted_iota(jnp.int32, (rows, rows), 0)
    col = lax.broadcasted_iota(jnp.int32, (rows, rows), 1)
    if n_sub == 1:
        causal = row >= col
        anti = row <= col
    else:
        same = (row // q) == (col // q)
        causal = jnp.logical_and(same, row >= col)
        anti = jnp.logical_and(same, row <= col)
    tril = jnp.where(causal, 1.0, 0.0).astype(BF16)
    triu = jnp.where(anti, 1.0, 0.0).astype(BF16)
    for gi in range(g_blk):
        _ssd_group(gi, causal, tril, triu, xs_ref, b_ref, c_ref, z_ref, dtc_ref, adtc_ref, adtr_ref, dsk_ref,
                   ng_ref, ex_ref, y_ref, s_ref, q=q, n_sub=n_sub, r_heads=r_heads, p_dim=p_dim)


def _ssd_group(gi, causal, tril, triu, xs_ref, b_ref, c_ref, z_ref, dtc_ref, adtc_ref, adtr_ref, dsk_ref,
               ng_ref, ex_ref, y_ref, s_ref, *, q, n_sub, r_heads, p_dim):
    rows = n_sub * q
    n_state = s_ref.shape[-1]
    rp = r_heads * p_dim
    ch = slice(gi * rp, (gi + 1) * rp)
    st = slice(gi * n_state, (gi + 1) * n_state)
    xs = xs_ref[:, ch]
    b_f = b_ref[:, st]
    c_f = c_ref[:, st]
    bm = b_f.astype(BF16)
    cm = c_f.astype(BF16)
    dt_c = dtc_ref[gi]
    adt_c = adtc_ref[gi]
    adt_r = adtr_ref[gi]
    cs_c = _dot_exact_lhs(tril, adt_c)
    cs_r = _dot_exact_rhs(adt_r, triu)
    lasts = [cs_c[k * q + q - 1:k * q + q, :] for k in range(n_sub)]
    if n_sub == 1:
        cs_last = lasts[0]
    else:
        cs_last = jnp.concatenate([jnp.broadcast_to(v, (q, r_heads)) for v in lasts], axis=0)

    stacked = jnp.concatenate([dt_c, cs_c, cs_last], axis=0) if n_sub > 1 else jnp.concatenate([dt_c, cs_c], axis=0)
    wide = _dot_exact_rhs(stacked, ex_ref[...])
    dt_x, cs_x = wide[0:rows], wide[rows:2 * rows]
    cs_last_x = wide[2 * rows:3 * rows] if n_sub > 1 else cs_x[rows - 1:rows, :]
    ecs_x = jnp.exp(cs_x)
    dec_x = jnp.exp(cs_last_x - cs_x)

    xdt = xs * dt_x
    xb = xdt.astype(BF16)
    xdec = xdt * dec_x

    y_state = []
    for k in range(n_sub):
        seq = slice(k * q, (k + 1) * q)
        s_old = s_ref[k, gi]
        y_state.append(lax.dot_general(c_f[seq].astype(BF16), s_old.astype(BF16), (((1,), (1,)), ((), ())),
                                       preferred_element_type=F32))
        upd = lax.dot_general(xdec[seq].astype(BF16), b_f[seq].astype(BF16), (((0,), (0,)), ((), ())),
                              preferred_element_type=F32)
        e_heads = jnp.broadcast_to(jnp.exp(cs_r[:, k * q + q - 1:k * q + q]), (r_heads, n_state))
        for h in range(r_heads):
            hp = slice(h * p_dim, (h + 1) * p_dim)
            s_ref[k, gi, hp, :] = s_old[hp] * e_heads[h:h + 1, :] + upd[hp]
    y = (y_state[0] if n_sub == 1 else jnp.concatenate(y_state, axis=0)) * ecs_x

    cb = lax.dot_general(cm, bm, (((1,), (1,)), ((), ())), preferred_element_type=F32)
    pieces = []
    heads_per_blk = max(1, LANE // p_dim)
    assert r_heads % heads_per_blk == 0
    for blk in range(r_heads // heads_per_blk):
        width = heads_per_blk * p_dim
        xblk = xb[:, blk * width:(blk + 1) * width]
        lane = lax.broadcasted_iota(jnp.int32, (rows, width), 1)
        acc = None
        for k in range(heads_per_blk):
            h = blk * heads_per_blk + k
            diff = cs_c[:, h:h + 1] - cs_r[h:h + 1, :]
            lmat = jnp.exp(jnp.where(causal, diff, -jnp.inf))
            yk = jnp.dot((cb * lmat).astype(BF16), xblk, preferred_element_type=F32)
            acc = yk if acc is None else jnp.where(lane >= k * p_dim, yk, acc)
        pieces.append(acc)
    y = y + (pieces[0] if len(pieces) == 1 else jnp.concatenate(pieces, axis=1))

    y = y + dsk_ref[:, ch] * xs
    yz = y * _silu(z_ref[:, ch].astype(F32))
    ms = jnp.mean(yz * yz, axis=-1, keepdims=True)
    y_ref[:, ch] = (yz * lax.rsqrt(ms + RMS_EPS) * ng_ref[:, ch]).astype(y_ref.dtype)


def _ssd_scan(xbc, z, dt, adt, d_skip_x, norm_g, s0, layer, y_buf, s_buf, *, m, n_layers, row0, n_seq,
              seq_len, n_sub, g_blk, groups, r_heads, p_dim, n_state):
    q = SSD_CHUNK if seq_len % SSD_CHUNK == 0 else seq_len
    nc = seq_len // q
    rp = r_heads * p_dim
    d_inner = groups * rp
    rows_blk = n_sub * q
    assert nc == 1 or n_sub == 1
    assert row0 % rows_blk == 0 and n_seq % n_sub == 0 and rows_blk % BF16_ROWS == 0
    assert rp % LANE == 0 and n_state % LANE == 0 and d_inner % n_state == 0
    blk0 = row0 // rows_blk
    rows = n_seq * seq_len

    def per_group(v):
        v = v[row0:row0 + rows].reshape(rows // rows_blk, rows_blk, groups, r_heads)
        return v.transpose(2, 0, 1, 3), v.transpose(2, 0, 3, 1)

    dt_c, _ = per_group(dt)
    adt_c, adt_r = per_group(adt)
    head_of = np.arange(rp) // p_dim
    expand = jnp.asarray(head_of[None, :] == np.arange(r_heads)[:, None], dtype=BF16)

    assert groups % g_blk == 0 and (d_inner // n_state) % g_blk == 0
    b_col0 = d_inner // (g_blk * n_state)
    c_col0 = b_col0 + groups // g_blk
    tok = lambda s, g, c: (blk0 + s * nc + c, g)
    per_chunk = lambda s, g, c: (g, s * nc + c, 0, 0)
    const = lambda s, g, c: (0, 0)
    in_specs = [
        pl.BlockSpec((rows_blk, g_blk * rp), tok),
        pl.BlockSpec((rows_blk, g_blk * n_state), lambda s, g, c: (blk0 + s * nc + c, b_col0 + g)),
        pl.BlockSpec((rows_blk, g_blk * n_state), lambda s, g, c: (blk0 + s * nc + c, c_col0 + g)),
        pl.BlockSpec((rows_blk, g_blk * rp), tok),
        pl.BlockSpec((g_blk, None, rows_blk, r_heads), per_chunk),
        pl.BlockSpec((g_blk, None, rows_blk, r_heads), per_chunk),
        pl.BlockSpec((g_blk, None, r_heads, rows_blk), per_chunk),
        pl.BlockSpec((1, g_blk * rp), lambda s, g, c: (0, g)),
        pl.BlockSpec((1, g_blk * rp), lambda s, g, c: (0, g)),
        pl.BlockSpec((r_heads, rp), const),
    ]
    args = [xbc, xbc, xbc, z, dt_c, adt_c, adt_r, d_skip_x, norm_g.reshape(1, d_inner), expand]
    state_spec = pl.BlockSpec((None, n_sub, g_blk, rp, n_state), lambda s, g, c: (layer, s, g, 0, 0))
    if s0 is not None:
        in_specs.append(state_spec)
        args.append(s0.reshape(n_layers, n_seq, groups, rp, n_state))
    aliases = {}
    for out_idx, buf in enumerate((y_buf, s_buf)):
        if buf is not None:
            aliases[len(args)] = out_idx
            in_specs.append(pl.BlockSpec(memory_space=pl.ANY))
            args.append(buf)
    y, s_new = pl.pallas_call(
        functools.partial(_ssd_kernel, q=q, n_sub=n_sub, g_blk=g_blk, r_heads=r_heads, p_dim=p_dim,
                          zero_init=s0 is None, n_alias=len(aliases)),
        grid=(n_seq // n_sub, groups // g_blk, nc),
        in_specs=in_specs,
        out_specs=[pl.BlockSpec((rows_blk, g_blk * rp), tok), state_spec],
        out_shape=[jax.ShapeDtypeStruct((m, d_inner), BF16),
                   jax.ShapeDtypeStruct((n_layers, n_seq, groups, rp, n_state), F32)],
        input_output_aliases=aliases,
        compiler_params=_params(("parallel", "parallel", "arbitrary")),
        name="ssd_scan_q%d" % q,
    )(*args)
    return y, s_new


def _tails(u, g, k):
    c = u.shape[1]
    last = np.arange(-k, 0)
    rows_p = ((np.arange(g.batch)[:, None] + 1) * g.seq + last[None, :]).reshape(-1)
    rows_s = (g.m_p + (np.arange(g.dec_batch)[:, None] + 1) * g.dec_seq + last[None, :]).reshape(-1)
    p = jnp.take(u, jnp.asarray(rows_p, jnp.int32), axis=0).reshape(g.batch, k, c)
    s = jnp.take(u, jnp.asarray(rows_s, jnp.int32), axis=0).reshape(g.dec_batch, k, c)
    return p, s


def kernel(x_prompt, x_sample, state_sconv, state_ssd_conv, state_ssd, c_prompt, c_sample, w_ada, b_ada, ln_g, ln_b, sc_w_in, sc_conv_w, sc_w_out, ssd_w_in, ssd_conv_w, ssd_conv_b, ssd_dt_bias, ssd_a_log, ssd_d, ssd_norm_g, ssd_w_out, ffn_w_gate, ffn_w_up, ffn_w_down):
    batch, seq, d = x_prompt.shape
    dec_batch, dec_seq, _ = x_sample.shape
    depth = w_ada.shape[0]
    g = _Slab(batch, seq, dec_batch, dec_seq, d)
    alpha = (2 * depth) ** 0.25

    n_ssd, _, heads, p_dim, n_state = state_ssd.shape
    d_inner = heads * p_dim
    conv_dim = ssd_conv_w.shape[-1]
    groups = (conv_dim - d_inner) // (2 * n_state)
    r_heads = heads // groups
    ffn_hidden = ffn_w_gate.shape[-1]

    n_c = batch + dec_batch
    n_c_pad = -(-n_c // BF16_ROWS) * BF16_ROWS
    c_all = jnp.concatenate([c_sample, c_prompt], axis=0)
    c_act = jnp.pad(_silu(c_all), ((0, n_c_pad - n_c), (0, 0))).astype(BF16)
    mod = _adaln(c_act, w_ada, b_ada)
    mod_p = [mod[l, dec_batch:n_c].reshape(batch, 1, N_MOD * d) for l in range(depth)]

    mix_in_b, mix_out_b = sc_w_in[0].astype(BF16), sc_w_out[0].astype(BF16)
    w_gate_b, w_up_b = ffn_w_gate[0].astype(BF16), ffn_w_up[0].astype(BF16)

    x, h = _modulate(g, x_prompt.reshape(g.m_p, d), x_sample.reshape(g.m_s, d), mod, mod_p, 0, 1, 0)

    sc_p, sc_s, cv_p, cv_s = [], [], [], []
    st_p = st_s = None
    for i in range(depth):
        j = i // 2
        if i % 2 == 0:
            gate_b, u = _matmul(h, [mix_in_b] * 3, [0, d, 2 * d], d, [BF16, F32],
                                epilogue="sconv", tm=1024, tn=512, tk=d, name="sconv_in")
            y = _causal_conv(g, u, state_sconv[j], sc_conv_w[j], None, gate_b, BF16, "sconv_conv")
            tp, ts = _tails(u, g, sc_conv_w.shape[1] - 1)
            sc_p.append(tp)
            sc_s.append(ts)
            (out,) = _matmul(y, [mix_out_b], [0], d, [BF16],
                             epilogue="plain", tm=1024, tn=1024, tk=4096, name="sconv_out")
        else:
            (z,) = _matmul(h, [mix_in_b], [0], d_inner, [BF16],
                           epilogue="plain", tm=1024, tn=1024, tk=4096, name="ssd_in_z")
            (xbc,) = _matmul(h, [mix_in_b], [d_inner], conv_dim, [F32],
                             epilogue="plain", tm=1024, tn=1024, tk=4096, name="ssd_in_xbc")
            (dt_raw,) = _matmul(h, [mix_in_b], [d_inner + conv_dim], heads, [F32],
                                epilogue="plain", tm=1024, tn=heads, tk=4096, name="ssd_in_dt")
            tp, ts = _tails(xbc, g, ssd_conv_w.shape[1] - 1)
            cv_p.append(tp)
            cv_s.append(ts)
            xbc_act = _causal_conv(g, xbc, state_ssd_conv[j], ssd_conv_w[j], ssd_conv_b[j], None, F32, "ssd_conv")
            dt, adt = _dt_prep(dt_raw, ssd_dt_bias[j], ssd_a_log[j])
            d_skip_x = jnp.repeat(ssd_d[j], p_dim).reshape(1, d_inner)
            common = dict(m=g.m, n_layers=n_ssd, groups=groups, r_heads=r_heads, p_dim=p_dim, n_state=n_state)
            y, st_p = _ssd_scan(xbc_act, z, dt, adt, d_skip_x, ssd_norm_g[j], None, j, None, st_p,
                                row0=0, n_seq=batch, seq_len=seq, n_sub=1,
                                g_blk=_pick(groups, SSD_PROMPT_GROUPS_PER_STEP, 1), **common)
            n_sub = _pick(dec_batch, max(1, SSD_CHUNK // dec_seq), 1) if dec_seq % SSD_CHUNK else 1
            y, st_s = _ssd_scan(xbc_act, z, dt, adt, d_skip_x, ssd_norm_g[j], state_ssd, j, y, st_s,
                                row0=g.m_p, n_seq=dec_batch, seq_len=dec_seq, n_sub=n_sub, g_blk=1, **common)
            (out,) = _matmul(y, [mix_out_b], [0], d, [BF16],
                             epilogue="plain", tm=1024, tn=1024, tk=4096, name="ssd_out")
        x, h = _ln_mod(g, x, out, mod, mod_p, i, 2, ln_g[i, 0], ln_b[i, 0], alpha, (i, 4, 3))

        more = i + 1 < depth
        side = [(ffn_w_down, i)] + ([(ffn_w_gate, i + 1), (ffn_w_up, i + 1)] if more else [])
        hid, w_down_b, *nxt_ffn = _matmul(h, [w_gate_b, w_up_b], [0, 0], ffn_hidden, [BF16], side=side,
                                          epilogue="swiglu", tm=1024, tn=512, tk=d, name="ffn_in",
                                          ragged_n=True)
        if more:
            stacks = (ssd_w_in, ssd_w_out) if (i + 1) % 2 else (sc_w_in, sc_w_out)
            side = [(w, (i + 1) // 2) for w in stacks]
        else:
            side = []
        out, *nxt_mix = _matmul(hid, [w_down_b], [0], d, [BF16], side=side,
                                epilogue="plain", tm=1024, tn=512, tk=5632, name="ffn_out")
        if more:
            w_gate_b, w_up_b = nxt_ffn
            mix_in_b, mix_out_b = nxt_mix
        nxt = (i + 1, 1, 0) if more else None
        x, h = _ln_mod(g, x, out, mod, mod_p, i, 5, ln_g[i, 1], ln_b[i, 1], alpha, nxt)

    rows_prompt, rows_sample = x, h
    y_prompt = rows_prompt.reshape(batch, seq, d)
    y_sample = rows_sample.reshape(dec_batch, dec_seq, d)
    st_p = st_p.reshape(n_ssd, batch, heads, p_dim, n_state)
    st_s = st_s.reshape(n_ssd, dec_batch, heads, p_dim, n_state)
    return (y_prompt, y_sample, jnp.stack(sc_p), jnp.stack(cv_p), st_p,
            jnp.stack(sc_s), jnp.stack(cv_s), st_s)
```
